```python
import jax
import jax.numpy as jnp
from jax import lax
import numpy as np


D_MODEL = 2048
BATCH = 4
SEQ = 2048
DEPTH = 4

CHUNK = 64
N_MIXERS = 3
Q_BLOCK = 128
EPS = 1e-6

DSA_HEADS = 16
DSA_LATENT = 256
IDX_HEADS = 16
IDX_DIM = 64
TOPK_MAX = 256
DSA_IN = DSA_HEADS * DSA_LATENT + DSA_LATENT + IDX_HEADS * IDX_DIM + IDX_DIM + IDX_HEADS

FOX_HEADS = 16
FOX_HEAD_DIM = D_MODEL // FOX_HEADS
FOX_IN = 4 * D_MODEL + FOX_HEADS

HGRN_EXPAND = 128
HGRN_HEADS = D_MODEL // HGRN_EXPAND
HGRN_FDIM = HGRN_HEADS * HGRN_EXPAND
HGRN_VDIM = D_MODEL // HGRN_HEADS
HGRN_IN = 2 * HGRN_FDIM + 2 * D_MODEL
HGRN_CHUNK = 64

D_FF = 5632
CONV_WIDTH = 3

N_DSA = (DEPTH + 2) // 3
N_FOX = (DEPTH + 1) // 3
N_HGRN = DEPTH // 3

kernel_name = 'hybrid_dsa_fox_hgrn2_convffn_adaln'


def rms_norm(x, g):
    xf = x.astype(jnp.float32)
    y = xf * lax.rsqrt(jnp.mean(xf * xf, axis=-1, keepdims=True) + EPS)
    return (y * g.astype(jnp.float32)).astype(x.dtype)


def modulate(h, shift, scale):
    return h * (1 + scale[:, None, :]) + shift[:, None, :]


def alibi_slopes(n):
    return jnp.exp2(-8.0 * jnp.arange(1, n + 1, dtype=jnp.float32) / n)


def dsa_mixer(h, w_in, q_gain, kv_gain, w_out):
    B, S, _ = h.shape
    proj = h @ w_in
    cuts = np.cumsum([DSA_HEADS * DSA_LATENT, DSA_LATENT, IDX_HEADS * IDX_DIM, IDX_DIM])
    q, lat, qi, ki, wi = jnp.split(proj, cuts, axis=-1)
    q = rms_norm(q.reshape(B, S, DSA_HEADS, DSA_LATENT), q_gain)
    lat = rms_norm(lat, kv_gain)
    qi = qi.reshape(B, S, IDX_HEADS, IDX_DIM)
    wi = wi * (IDX_HEADS ** -0.5 * IDX_DIM ** -0.5)
    topk = min(TOPK_MAX, S // 4)
    pos = jnp.arange(S)
    slopes = alibi_slopes(DSA_HEADS)
    nb = S // Q_BLOCK

    def to_blocks(t):
        return t.reshape((B, nb, Q_BLOCK) + t.shape[2:]).swapaxes(0, 1)

    def attend(args):
        qb, qib, wb, tq = args
        sc = jnp.einsum('bqhd,bsd->bqhs', qib, ki)
        score = jnp.einsum('bqh,bqhs->bqs', wb, jax.nn.relu(sc)).astype(jnp.float32)
        allowed = (pos // CHUNK)[None, :] <= (tq // CHUNK)[:, None]
        score = jnp.where(allowed[None], score, -jnp.inf)
        _, sel = lax.top_k(score, topk)
        valid = (sel // CHUNK) <= (tq // CHUNK)[None, :, None]
        kv = jax.vmap(lambda l, s: l[s])(lat, sel)
        logits = jnp.einsum('bqhr,bqkr->bqhk', qb, kv).astype(jnp.float32) * (DSA_LATENT ** -0.5)
        dist = jnp.abs(tq[None, :, None] - sel).astype(jnp.float32)
        logits = logits - slopes[None, None, :, None] * dist[:, :, None, :]
        logits = jnp.where(valid[:, :, None, :], logits, -jnp.inf)
        p = jax.nn.softmax(logits, axis=-1).astype(kv.dtype)
        return jnp.einsum('bqhk,bqkr->bqhr', p, kv)

    o = lax.map(attend, (to_blocks(q), to_blocks(qi), to_blocks(wi), pos.reshape(nb, Q_BLOCK)))
    o = o.swapaxes(0, 1).reshape(B, S, DSA_HEADS * DSA_LATENT)
    return o @ w_out


def fox_mixer(h, w_in, b_f, q_gain, k_gain, w_out):
    B, S, D = h.shape
    proj = h @ w_in
    q, k, v, g, fl = jnp.split(proj, [D, 2 * D, 3 * D, 4 * D], axis=-1)
    q = rms_norm(q.reshape(B, S, FOX_HEADS, FOX_HEAD_DIM), q_gain)
    k = rms_norm(k.reshape(B, S, FOX_HEADS, FOX_HEAD_DIM), k_gain)
    v = v.reshape(B, S, FOX_HEADS, FOX_HEAD_DIM)
    logf = jax.nn.log_sigmoid((fl + b_f).astype(jnp.float32))
    cum = jnp.cumsum(logf, axis=1)
    cum_t = cum.transpose(0, 2, 1)
    pos = jnp.arange(S)
    nb = S // Q_BLOCK

    def to_blocks(t):
        return t.reshape((B, nb, Q_BLOCK) + t.shape[2:]).swapaxes(0, 1)

    def attend(args):
        qb, cq, tq = args
        logits = jnp.einsum('bqhd,bshd->bhqs', qb, k).astype(jnp.float32) * (FOX_HEAD_DIM ** -0.5)
        logits = logits + cq.transpose(0, 2, 1)[..., None] - cum_t[:, :, None, :]
        logits = jnp.where((pos[None, :] <= tq[:, None])[None, None], logits, -jnp.inf)
        p = jax.nn.softmax(logits, axis=-1).astype(v.dtype)
        return jnp.einsum('bhqs,bshd->bqhd', p, v)

    o = lax.map(attend, (to_blocks(q), to_blocks(cum), pos.reshape(nb, Q_BLOCK)))
    o = o.swapaxes(0, 1).reshape(B, S, D)
    return (o * jax.nn.sigmoid(g)) @ w_out


def hgrn2_mixer(h, w_in, lb, o_gain, w_out):
    B, S, D = h.shape
    proj = h @ w_in
    q, fx, i, g = jnp.split(proj, [HGRN_FDIM, 2 * HGRN_FDIM, 2 * HGRN_FDIM + D], axis=-1)
    q = (jax.nn.silu(q) * (HGRN_EXPAND ** -0.5)).astype(jnp.float32)
    f = lb + (1 - lb) * jax.nn.sigmoid(fx.astype(jnp.float32))
    logf = jnp.log(f)
    k = 1 - f
    nc = S // HGRN_CHUNK

    def to_chunks(t, d):
        return t.reshape(B, nc, HGRN_CHUNK, HGRN_HEADS, d).transpose(1, 0, 3, 2, 4)

    qc = to_chunks(q, HGRN_EXPAND)
    kc = to_chunks(k, HGRN_EXPAND)
    gc = to_chunks(logf, HGRN_EXPAND)
    vc = to_chunks(i.astype(jnp.float32), HGRN_VDIM)
    causal = jnp.tril(jnp.ones((HGRN_CHUNK, HGRN_CHUNK), dtype=bool))

    def step(state, inp):
        qt, kt, vt, gt = inp
        G = jnp.cumsum(gt, axis=2)
        o_inter = jnp.einsum('bhtd,bhde->bhte', qt * jnp.exp(G), state)
        diff = G[:, :, :, None, :] - G[:, :, None, :, :]
        decay = jnp.exp(jnp.where(causal[:, :, None], diff, -jnp.inf))
        scores = jnp.einsum('bhtd,bhsd,bhtsd->bhts', qt, kt, decay)
        o = o_inter + jnp.einsum('bhts,bhse->bhte', scores, vt)
        g_last = G[:, :, -1, :]
        new_state = jnp.exp(g_last)[..., None] * state + jnp.einsum(
            'bhsd,bhse->bhde', kt * jnp.exp(g_last[:, :, None, :] - G), vt)
        return new_state, o

    s0 = jnp.zeros((B, HGRN_HEADS, HGRN_EXPAND, HGRN_VDIM), jnp.float32)
    _, o = lax.scan(step, s0, (qc, kc, vc, gc))
    o = o.transpose(1, 0, 3, 2, 4).reshape(B, S, HGRN_HEADS, HGRN_VDIM).astype(h.dtype)
    o = rms_norm(o, o_gain).reshape(B, S, D) * jax.nn.silu(g)
    return o @ w_out


def conv_ffn(h, w_in, conv_w, conv_b, w_out):
    u, gt = jnp.split(h @ w_in, 2, axis=-1)
    gt = lax.conv_general_dilated(
        gt, conv_w[:, None, :], window_strides=(1,), padding=[(CONV_WIDTH - 1, 0)],
        dimension_numbers=('NWC', 'WIO', 'NWC'), feature_group_count=D_FF) + conv_b
    return (jax.nn.silu(gt) * u) @ w_out


def setup_inputs(seed: int = 0) -> dict:
    key = jax.random.key(seed)
    ks = iter(jax.random.split(key, 32))
    f32 = jnp.float32

    def w(shape, fan_in, scale=1.0):
        return jax.random.normal(next(ks), shape, f32) * (scale * fan_in ** -0.5)

    def gain(shape):
        return 1.0 + 0.05 * jax.random.normal(next(ks), shape, f32)

    def small(shape, s=0.01):
        return s * jax.random.normal(next(ks), shape, f32)

    D = D_MODEL
    return {
        'x': jax.random.normal(next(ks), (BATCH, SEQ, D), f32),
        'c': jax.random.normal(next(ks), (BATCH, D), f32),
        'ada_w': w((DEPTH, D, 6 * D), D, 0.5),
        'ada_b': small((DEPTH, 6 * D)),
        'norm_mix_g': gain((DEPTH, D)),
        'norm_ffn_g': gain((DEPTH, D)),
        'dsa_w_in': w((N_DSA, D, DSA_IN), D),
        'dsa_q_norm': gain((N_DSA, DSA_LATENT)),
        'dsa_kv_norm': gain((N_DSA, DSA_LATENT)),
        'dsa_w_out': w((N_DSA, DSA_HEADS * DSA_LATENT, D), DSA_HEADS * DSA_LATENT),
        'fox_w_in': w((N_FOX, D, FOX_IN), D),
        'fox_b_f': small((N_FOX, FOX_HEADS), 0.1),
        'fox_q_norm': gain((N_FOX, FOX_HEAD_DIM)),
        'fox_k_norm': gain((N_FOX, FOX_HEAD_DIM)),
        'fox_w_out': w((N_FOX, D, D), D),
        'hgrn_w_in': w((N_HGRN, D, HGRN_IN), D),
        'hgrn_lb': small((DEPTH, HGRN_FDIM), 0.1),
        'hgrn_o_norm': gain((N_HGRN, HGRN_VDIM)),
        'hgrn_w_out': w((N_HGRN, D, D), D),
        'ffn_w_in': w((DEPTH, D, 2 * D_FF), D),
        'ffn_conv_w': w((DEPTH, CONV_WIDTH, D_FF), CONV_WIDTH),
        'ffn_conv_b': small((DEPTH, D_FF)),
        'ffn_w_out': w((DEPTH, D_FF, D), D_FF),
    }


def reference(x, c, ada_w, ada_b, norm_mix_g, norm_ffn_g, dsa_w_in, dsa_q_norm, dsa_kv_norm,
              dsa_w_out, fox_w_in, fox_b_f, fox_q_norm, fox_k_norm, fox_w_out, hgrn_w_in,
              hgrn_lb, hgrn_o_norm, hgrn_w_out, ffn_w_in, ffn_conv_w, ffn_conv_b, ffn_w_out):
    cond = jax.nn.silu(c)
    lb_soft = jax.nn.softmax(hgrn_lb.astype(jnp.float32), axis=0)
    lb_all = jnp.cumsum(lb_soft, axis=0) - lb_soft[0]
    for i in range(DEPTH):
        mod = cond @ ada_w[i] + ada_b[i]
        sh1, sc1, g1, sh2, sc2, g2 = jnp.split(mod, 6, axis=-1)
        h = modulate(rms_norm(x, norm_mix_g[i]), sh1, sc1)
        kind, j = i % N_MIXERS, i // N_MIXERS
        if kind == 0:
            y = dsa_mixer(h, dsa_w_in[j], dsa_q_norm[j], dsa_kv_norm[j], dsa_w_out[j])
        elif kind == 1:
            y = fox_mixer(h, fox_w_in[j], fox_b_f[j], fox_q_norm[j], fox_k_norm[j], fox_w_out[j])
        else:
            y = hgrn2_mixer(h, hgrn_w_in[j], lb_all[i], hgrn_o_norm[j], hgrn_w_out[j])
        x = x + g1[:, None, :] * y
        h = modulate(rms_norm(x, norm_ffn_g[i]), sh2, sc2)
        x = x + g2[:, None, :] * conv_ffn(h, ffn_w_in[i], ffn_conv_w[i], ffn_conv_b[i], ffn_w_out[i])
    return x
```

```python
import functools

import jax
import jax.numpy as jnp
from jax import lax
from jax.experimental import pallas as pl
from jax.experimental.pallas import tpu as pltpu

F32 = jnp.float32
BF16 = jnp.bfloat16

EPS = 1e-6
CHUNK = 64
TOPK_MAX = 256

DSA_HEADS = 16
DSA_LATENT = 256
IDX_HEADS = 16
IDX_DIM = 64
FOX_HEADS = 16
HGRN_HEADS = 16
HEAD_DIM = 128
CONV_WIDTH = 3

NEG_BIG = -1e30
INT_MIN = -2 ** 31

V7X_VMEM_BYTES = 64 * 1024 * 1024
VMEM_LIMIT = 56 * 1024 * 1024


def _params(*sem):
    return pltpu.CompilerParams(dimension_semantics=sem, vmem_limit_bytes=VMEM_LIMIT)


def _dot(a, b):
    return jnp.dot(a, b, preferred_element_type=F32)


def _dot_nt(a, b):
    return lax.dot_general(a, b, (((1,), (1,)), ((), ())), preferred_element_type=F32)


def _dot_tn(a, b):
    return lax.dot_general(a, b, (((0,), (0,)), ((), ())), preferred_element_type=F32)


def _silu(x):
    return x * jax.nn.sigmoid(x)


def _ada_kernel(c_ref, w_ref, b_ref, o_ref):
    cond = _silu(c_ref[...]).astype(BF16)
    o_ref[0] = _dot(cond, w_ref[0].astype(BF16)) + b_ref[0]


def _ada_all(c, ada_w, ada_b, tn=1024):
    depth, d, n = ada_w.shape
    bsz = c.shape[0]
    rows = 8
    c_pad = jnp.zeros((rows, d), F32).at[:bsz].set(c)
    out = pl.pallas_call(
        _ada_kernel,
        grid=(depth, n // tn),
        in_specs=[
            pl.BlockSpec((rows, d), lambda l, j: (0, 0)),
            pl.BlockSpec((1, d, tn), lambda l, j: (l, 0, j)),
            pl.BlockSpec((1, 1, tn), lambda l, j: (l, 0, j)),
        ],
        out_specs=pl.BlockSpec((1, rows, tn), lambda l, j: (l, 0, j)),
        out_shape=jax.ShapeDtypeStruct((depth, rows, n), F32),
        compiler_params=_params("parallel", "parallel"),
        name="ada_mod",
    )(c_pad, ada_w, ada_b.reshape(depth, 1, n))
    return out[:, :bsz]


def _norm_mod_kernel(x_ref, g_ref, sh_ref, sc_ref, h_ref):
    x = x_ref[...]
    ms = jnp.mean(x * x, axis=-1, keepdims=True)
    y = x * lax.rsqrt(ms + EPS) * g_ref[...]
    h_ref[...] = (y * (1.0 + sc_ref[0]) + sh_ref[0]).astype(BF16)


def _norm_mod(x2, g, shift, scale, seq, tm=512):
    t, d = x2.shape
    per_seq = seq // tm
    bsz = shift.shape[0]
    vec = lambda m: (m // per_seq, 0, 0)
    return pl.pallas_call(
        _norm_mod_kernel,
        grid=(t // tm,),
        in_specs=[
            pl.BlockSpec((tm, d), lambda m: (m, 0)),
            pl.BlockSpec((1, d), lambda m: (0, 0)),
            pl.BlockSpec((1, 1, d), vec),
            pl.BlockSpec((1, 1, d), vec),
        ],
        out_specs=pl.BlockSpec((tm, d), lambda m: (m, 0)),
        out_shape=jax.ShapeDtypeStruct((t, d), BF16),
        compiler_params=_params("parallel"),
        name="norm_mod",
    )(x2, g.reshape(1, d), shift.reshape(bsz, 1, d), scale.reshape(bsz, 1, d))


def _proj_kernel(*refs, epilogue, n_extra):
    h_ref, w_ref = refs[0], refs[1]
    extras = refs[2:2 + n_extra]
    outs = refs[2 + n_extra:]
    acc = _dot(h_ref[...], w_ref[...])
    epilogue(acc, [e[...] for e in extras], outs)


def _proj(h, w, col0, n, epilogue, out_defs, extras=(), tm=1024, tn=1024):
    t, k = h.shape
    tn = min(tn, n)
    tm = min(tm, t)
    assert n % tn == 0 and col0 % tn == 0 and t % tm == 0
    off = col0 // tn
    in_specs = [
        pl.BlockSpec((tm, k), lambda j, m: (m, 0)),
        pl.BlockSpec((k, tn), lambda j, m: (0, j + off)),
    ] + [pl.BlockSpec((1, tn), lambda j, m: (0, j)) for _ in extras]
    out_specs = [pl.BlockSpec((tm, wt), lambda j, m: (m, j)) for wt, _, _ in out_defs]
    out_shape = [jax.ShapeDtypeStruct((t, wtot), dt) for _, wtot, dt in out_defs]
    return pl.pallas_call(
        functools.partial(_proj_kernel, epilogue=epilogue, n_extra=len(extras)),
        grid=(n // tn, t // tm),
        in_specs=in_specs,
        out_specs=out_specs,
        out_shape=out_shape,
        compiler_params=_params("parallel", "parallel"),
        name="proj_" + epilogue.__name__.strip("_"),
    )(h, w, *extras)


def _rms_heads(acc, gain_row, hd, scale, o_ref):
    for j in range(acc.shape[1] // hd):
        seg = acc[:, j * hd:(j + 1) * hd]
        ms = jnp.mean(seg * seg, axis=-1, keepdims=True)
        g = gain_row[:, j * hd:(j + 1) * hd]
        if scale != 1.0:
            g = g * scale
        o_ref[:, j * hd:(j + 1) * hd] = (seg * lax.rsqrt(ms + EPS) * g).astype(o_ref.dtype)


def _ep_dsa_q(acc, extras, outs):
    _rms_heads(acc, extras[0], DSA_LATENT, DSA_LATENT ** -0.5, outs[0])


def _ep_fox_q(acc, extras, outs):
    _rms_heads(acc, extras[0], HEAD_DIM, HEAD_DIM ** -0.5, outs[0])


def _ep_fox_k(acc, extras, outs):
    _rms_heads(acc, extras[0], HEAD_DIM, 1.0, outs[0])


def _ep_cast(acc, extras, outs):
    outs[0][...] = acc.astype(outs[0].dtype)


def _ep_sigmoid(acc, extras, outs):
    outs[0][...] = jax.nn.sigmoid(acc).astype(outs[0].dtype)


def _ep_silu(acc, extras, outs):
    outs[0][...] = _silu(acc).astype(outs[0].dtype)


def _ep_hgrn_q(acc, extras, outs):
    outs[0][...] = (_silu(acc) * (HEAD_DIM ** -0.5)).astype(outs[0].dtype)


def _ep_hgrn_logf(acc, extras, outs):
    lb = extras[0]
    outs[0][...] = jnp.log(lb + (1.0 - lb) * jax.nn.sigmoid(acc))


_DSA_S_LAT = 0
_DSA_S_QI = DSA_LATENT
_DSA_S_KE = _DSA_S_QI + IDX_HEADS * IDX_DIM
_DSA_S_KO = _DSA_S_KE + 2 * IDX_DIM
_DSA_S_WI = _DSA_S_KO + 2 * IDX_DIM
_DSA_S_N = _DSA_S_WI + 128


def _ep_dsa_small(acc, extras, outs):
    lat_ref, qi_ref, ke_ref, ko_ref, wi_ref = outs
    _rms_heads(acc[:, :DSA_LATENT], extras[0][:, :DSA_LATENT], DSA_LATENT, 1.0, lat_ref)
    qi_ref[...] = acc[:, _DSA_S_QI:_DSA_S_KE].astype(BF16)
    ke_ref[...] = acc[:, _DSA_S_KE:_DSA_S_KO].astype(BF16)
    ko_ref[...] = acc[:, _DSA_S_KO:_DSA_S_WI].astype(BF16)
    wi_ref[...] = acc[:, _DSA_S_WI:_DSA_S_WI + IDX_HEADS] * (IDX_HEADS ** -0.5 * IDX_DIM ** -0.5)


def _out_kernel(a_ref, w_ref, x_ref, g_ref, o_ref):
    o_ref[...] = x_ref[...] + g_ref[0] * _dot(a_ref[...], w_ref[...])


def _out_proj(a, w, x2, gate, seq, tm, tn=1024):
    t, k = a.shape
    d = w.shape[1]
    bsz = gate.shape[0]
    per_seq = seq // tm
    return pl.pallas_call(
        _out_kernel,
        grid=(d // tn, t // tm),
        in_specs=[
            pl.BlockSpec((tm, k), lambda j, m: (m, 0)),
            pl.BlockSpec((k, tn), lambda j, m: (0, j)),
            pl.BlockSpec((tm, tn), lambda j, m: (m, j)),
            pl.BlockSpec((1, 1, tn), lambda j, m: (m // per_seq, 0, j)),
        ],
        out_specs=pl.BlockSpec((tm, tn), lambda j, m: (m, j)),
        out_shape=jax.ShapeDtypeStruct((t, d), F32),
        compiler_params=_params("parallel", "parallel"),
        name="out_proj",
    )(a, w, x2, gate.reshape(bsz, 1, d))


def _ffn_in_kernel(h_ref, wu_ref, wg_ref, cw_ref, cb_ref, a_ref, carry_ref, *, per_seq):
    m = pl.program_id(1)
    h = h_ref[...]
    u = _dot(h, wu_ref[...])
    g = _dot(h, wg_ref[...])
    tm = g.shape[0]

    @pl.when(m % per_seq == 0)
    def _():
        carry_ref[...] = jnp.zeros_like(carry_ref)

    prev = carry_ref[...]
    p1 = prev[7:8, :]
    p2 = prev[6:7, :]
    row = lax.broadcasted_iota(jnp.int32, g.shape, 0)
    g1 = jnp.where(row == 0, p1, pltpu.roll(g, 1, axis=0))
    g2 = jnp.where(row == 0, p2, jnp.where(row == 1, p1, pltpu.roll(g, 2, axis=0)))
    carry_ref[...] = g[tm - 8:, :]
    cw = cw_ref[0]
    conv = cw[0:1] * g2 + cw[1:2] * g1 + cw[2:3] * g + cb_ref[0]
    a_ref[...] = (_silu(conv) * u).astype(BF16)


def _ffn_in(h, w, conv_w, conv_b, layer, seq, tm=1024, tn=512):
    t, k = h.shape
    f = w.shape[1] // 2
    nf = f // tn
    per_seq = seq // tm
    return pl.pallas_call(
        functools.partial(_ffn_in_kernel, per_seq=per_seq),
        grid=(nf, t // tm),
        in_specs=[
            pl.BlockSpec((tm, k), lambda j, m: (m, 0)),
            pl.BlockSpec((k, tn), lambda j, m: (0, j)),
            pl.BlockSpec((k, tn), lambda j, m: (0, j + nf)),
            pl.BlockSpec((1, CONV_WIDTH, tn), lambda j, m: (layer, 0, j)),
            pl.BlockSpec((1, 1, tn), lambda j, m: (layer, 0, j)),
        ],
        out_specs=pl.BlockSpec((tm, tn), lambda j, m: (m, j)),
        out_shape=jax.ShapeDtypeStruct((t, f), BF16),
        scratch_shapes=[pltpu.VMEM((8, tn), F32)],
        compiler_params=_params("parallel", "arbitrary"),
        name="ffn_in",
    )(h, w, w, conv_w, conv_b)


def _dsa_select_kernel(qi_ref, ke_ref, ko_ref, wi_ref, bias_ref, key_ref, *, topk, s_tile):
    i = pl.program_id(1)
    tq = qi_ref.shape[0]
    seq = ke_ref.shape[0]
    qi = qi_ref[...]
    lhs = jnp.concatenate([qi[:, 128 * j:128 * (j + 1)] for j in range(IDX_HEADS // 2)], axis=0)
    wi = wi_ref[...]
    tpos = i * tq + lax.broadcasted_iota(jnp.int32, (tq, s_tile), 0)
    tchunk_end = (tpos // CHUNK + 1) * CHUNK
    for st in range(seq // s_tile):
        ke = ke_ref[st * s_tile:(st + 1) * s_tile, :]
        ko = ko_ref[st * s_tile:(st + 1) * s_tile, :]
        sce = _dot_nt(lhs, ke)
        sco = _dot_nt(lhs, ko)
        score = jnp.zeros((tq, s_tile), F32)
        for j in range(IDX_HEADS // 2):
            score = score + wi[:, 2 * j:2 * j + 1] * jnp.maximum(sce[j * tq:(j + 1) * tq], 0.0)
            score = score + wi[:, 2 * j + 1:2 * j + 2] * jnp.maximum(sco[j * tq:(j + 1) * tq], 0.0)
        score = score + 0.0
        bits = pltpu.bitcast(score, jnp.int32)
        key = jnp.where(bits < 0, bits ^ jnp.int32(0x7FFFFFFF), bits)
        spos = st * s_tile + lax.broadcasted_iota(jnp.int32, (tq, s_tile), 1)
        key_ref[:, st * s_tile:(st + 1) * s_tile] = jnp.where(spos < tchunk_end, key, INT_MIN)

    keys = key_ref[...]
    kf = jnp.float32(topk)

    def count_ge(cand):
        return jnp.sum(jnp.where(keys >= cand, 1.0, 0.0), axis=-1, keepdims=True)

    thr = jnp.where(count_ge(jnp.zeros((tq, 1), jnp.int32)) >= kf, 0, INT_MIN).astype(jnp.int32)

    def body(b, thr):
        cand = thr + lax.shift_left(jnp.int32(1), jnp.int32(30) - b)
        return jnp.where(count_ge(cand) >= kf, cand, thr)

    thr = lax.fori_loop(0, 31, body, thr)
    sel = (keys >= thr) & (keys > INT_MIN)
    bias_ref[0] = jnp.where(sel, 0.0, NEG_BIG).astype(BF16)


def _dsa_select(qi, ke, ko, wi, bsz, seq, tq=128, s_tile=512):
    nq = seq // tq
    topk = min(TOPK_MAX, seq // 4)
    return pl.pallas_call(
        functools.partial(_dsa_select_kernel, topk=topk, s_tile=s_tile),
        grid=(bsz, nq),
        in_specs=[
            pl.BlockSpec((tq, IDX_HEADS * IDX_DIM), lambda b, i: (b * nq + i, 0)),
            pl.BlockSpec((seq, 2 * IDX_DIM), lambda b, i: (b, 0)),
            pl.BlockSpec((seq, 2 * IDX_DIM), lambda b, i: (b, 0)),
            pl.BlockSpec((tq, IDX_HEADS), lambda b, i: (b * nq + i, 0)),
        ],
        out_specs=pl.BlockSpec((1, tq, seq), lambda b, i: (b, i, 0)),
        out_shape=jax.ShapeDtypeStruct((bsz, seq, seq), BF16),
        scratch_shapes=[pltpu.VMEM((tq, seq), jnp.int32)],
        compiler_params=_params("parallel", "parallel"),
        name="dsa_select",
    )(qi, ke, ko, wi)


def _dsa_attn_kernel(q_ref, lat_ref, bias_ref, o_ref, m_ref, l_ref, acc_ref, *, hg, tk):
    i = pl.program_id(1)
    tq = q_ref.shape[0]
    r = DSA_LATENT
    n_tiles = ((i + 1) * tq + tk - 1) // tk
    rows = hg * tq
    for g in range(DSA_HEADS // hg):
        heads = range(g * hg, (g + 1) * hg)
        q4 = jnp.concatenate([q_ref[:, h * r:(h + 1) * r] for h in heads], axis=0)
        m_ref[...] = jnp.full((rows, 1), NEG_BIG, F32)
        l_ref[...] = jnp.zeros((rows, 1), F32)
        acc_ref[...] = jnp.zeros((rows, r), F32)

        def body(j, carry):
            k0 = pl.multiple_of(j * tk, tk)
            lat = lat_ref[pl.ds(k0, tk), :]
            s = _dot_nt(q4, lat)
            tpos = i * tq + lax.broadcasted_iota(jnp.int32, (tq, tk), 0)
            spos = k0 + lax.broadcasted_iota(jnp.int32, (tq, tk), 1)
            dist = jnp.abs(tpos - spos).astype(F32)
            bias = bias_ref[0, :, pl.ds(k0, tk)].astype(F32)
            base = jnp.concatenate(
                [bias - (2.0 ** (-8.0 * (h + 1) / DSA_HEADS)) * dist for h in heads], axis=0)
            s = s + base
            m_old = m_ref[...]
            m_new = jnp.maximum(m_old, jnp.max(s, axis=-1, keepdims=True))
            alpha = jnp.exp(m_old - m_new)
            p = jnp.exp(s - m_new)
            l_ref[...] = alpha * l_ref[...] + jnp.sum(p, axis=-1, keepdims=True)
            acc_ref[...] = alpha * acc_ref[...] + _dot(p.astype(BF16), lat)
            m_ref[...] = m_new
            return carry

        lax.fori_loop(0, n_tiles, body, 0)
        out = acc_ref[...] / l_ref[...]
        for idx, h in enumerate(heads):
            o_ref[:, h * r:(h + 1) * r] = out[idx * tq:(idx + 1) * tq].astype(BF16)


def _dsa_attn(q, lat, bias, bsz, seq, tq=128, hg=4, tk=256):
    nq = seq // tq
    width = DSA_HEADS * DSA_LATENT
    return pl.pallas_call(
        functools.partial(_dsa_attn_kernel, hg=hg, tk=tk),
        grid=(bsz, nq),
        in_specs=[
            pl.BlockSpec((tq, width), lambda b, i: (b * nq + i, 0)),
            pl.BlockSpec((seq, DSA_LATENT), lambda b, i: (b, 0)),
            pl.BlockSpec((1, tq, seq), lambda b, i: (b, i, 0)),
        ],
        out_specs=pl.BlockSpec((tq, width), lambda b, i: (b * nq + i, 0)),
        out_shape=jax.ShapeDtypeStruct((bsz * seq, width), BF16),
        scratch_shapes=[
            pltpu.VMEM((hg * tq, 1), F32),
            pltpu.VMEM((hg * tq, 1), F32),
            pltpu.VMEM((hg * tq, DSA_LATENT), F32),
        ],
        compiler_params=_params("parallel", "parallel"),
        name="dsa_attn",
    )(q, lat, bias)


def _split3(x):
    hi = x.astype(BF16)
    r1 = x - hi.astype(F32)
    mid = r1.astype(BF16)
    lo = (r1 - mid.astype(F32)).astype(BF16)
    return hi, mid, lo


def _fox_cum_kernel(h_ref, wt_ref, b_ref, cum_ref, *, blk):
    fl = _dot_nt(wt_ref[...], h_ref[...]) + b_ref[...]
    logf = jax.nn.log_sigmoid(fl)
    seq = logf.shape[1]
    r = lax.broadcasted_iota(jnp.int32, (blk, blk), 0)
    c = lax.broadcasted_iota(jnp.int32, (blk, blk), 1)
    upper = jnp.where(r <= c, 1.0, 0.0).astype(BF16)
    carry = jnp.zeros((logf.shape[0], 1), F32)
    for j in range(seq // blk):
        hi, mid, lo = _split3(logf[:, j * blk:(j + 1) * blk])
        cs = (_dot(hi, upper) + _dot(mid, upper)) + _dot(lo, upper) + carry
        cum_ref[0, :, j * blk:(j + 1) * blk] = cs
        carry = cs[:, blk - 1:blk]


def _fox_cum(h, w_fl_t, b_f, bsz, seq, blk=256):
    d = h.shape[1]
    return pl.pallas_call(
        functools.partial(_fox_cum_kernel, blk=blk),
        grid=(bsz,),
        in_specs=[
            pl.BlockSpec((seq, d), lambda b: (b, 0)),
            pl.BlockSpec((FOX_HEADS, d), lambda b: (0, 0)),
            pl.BlockSpec((FOX_HEADS, 1), lambda b: (0, 0)),
        ],
        out_specs=pl.BlockSpec((1, FOX_HEADS, seq), lambda b: (b, 0, 0)),
        out_shape=jax.ShapeDtypeStruct((bsz, FOX_HEADS, seq), F32),
        compiler_params=_params("parallel"),
        name="fox_cum",
    )(h, w_fl_t, b_f.reshape(FOX_HEADS, 1))


def _fox_attn_kernel(q_ref, k_ref, v_ref, g_ref, cum_ref, cumq_ref, o_ref, m_ref, l_ref, acc_ref, *, tk):
    i = pl.program_id(2)
    tq = q_ref.shape[0]
    q = q_ref[...]
    n_tiles = ((i + 1) * tq + tk - 1) // tk
    c0 = cumq_ref[0, :, 0:1]
    m_ref[...] = jnp.full((tq, 1), NEG_BIG, F32)
    l_ref[...] = jnp.zeros((tq, 1), F32)
    acc_ref[...] = jnp.zeros((tq, HEAD_DIM), F32)

    def body(j, carry):
        k0 = pl.multiple_of(j * tk, tk)
        k = k_ref[pl.ds(k0, tk), :]
        v = v_ref[pl.ds(k0, tk), :]
        s = _dot_nt(q, k) + (c0 - cum_ref[0, :, pl.ds(k0, tk)])
        tpos = i * tq + lax.broadcasted_iota(jnp.int32, (tq, tk), 0)
        spos = k0 + lax.broadcasted_iota(jnp.int32, (tq, tk), 1)
        s = jnp.where(spos <= tpos, s, NEG_BIG)
        m_old = m_ref[...]
        m_new = jnp.maximum(m_old, jnp.max(s, axis=-1, keepdims=True))
        alpha = jnp.exp(m_old - m_new)
        p = jnp.exp(s - m_new)
        l_ref[...] = alpha * l_ref[...] + jnp.sum(p, axis=-1, keepdims=True)
        acc_ref[...] = alpha * acc_ref[...] + _dot(p.astype(BF16), v)
        m_ref[...] = m_new
        return carry

    lax.fori_loop(0, n_tiles, body, 0)
    o = acc_ref[...] / l_ref[...]
    o_ref[...] = (o * g_ref[...].astype(F32)).astype(BF16)


def _fox_attn(q, k, v, gate, cum, bsz, seq, tq=256, tk=256):
    nq = seq // tq
    d = q.shape[1]
    cum3 = cum.reshape(bsz * FOX_HEADS, 1, seq)
    qmap = lambda b, h, i: (b * nq + i, h)
    kvmap = lambda b, h, i: (b, h)
    return pl.pallas_call(
        functools.partial(_fox_attn_kernel, tk=tk),
        grid=(bsz, FOX_HEADS, nq),
        in_specs=[
            pl.BlockSpec((tq, HEAD_DIM), qmap),
            pl.BlockSpec((seq, HEAD_DIM), kvmap),
            pl.BlockSpec((seq, HEAD_DIM), kvmap),
            pl.BlockSpec((tq, HEAD_DIM), qmap),
            pl.BlockSpec((1, 1, seq), lambda b, h, i: (b * FOX_HEADS + h, 0, 0)),
            pl.BlockSpec((1, 1, tq), lambda b, h, i: (b * FOX_HEADS + h, 0, i)),
        ],
        out_specs=pl.BlockSpec((tq, HEAD_DIM), qmap),
        out_shape=jax.ShapeDtypeStruct((bsz * seq, d), BF16),
        scratch_shapes=[
            pltpu.VMEM((tq, 1), F32),
            pltpu.VMEM((tq, 1), F32),
            pltpu.VMEM((tq, HEAD_DIM), F32),
        ],
        compiler_params=_params("parallel", "parallel", "parallel"),
        name="fox_attn",
    )(q, k, v, gate, cum3, cum3)


HGRN_SUB = 16


def _hgrn_kernel(q_ref, lg_ref, v_ref, gate_ref, gain_ref, o_ref,
                 state_ref, oacc_ref, kpad_ref, gpad_ref, vpad_ref, *, hb):
    c = pl.program_id(2)
    cl = q_ref.shape[0]
    dk = HEAD_DIM
    sub = HGRN_SUB

    @pl.when(c == 0)
    def _():
        state_ref[...] = jnp.zeros_like(state_ref)
        kpad_ref[0:sub, :] = jnp.zeros((sub, dk), F32)
        gpad_ref[0:sub, :] = jnp.zeros((sub, dk), F32)
        vpad_ref[0:sub, :] = jnp.zeros((sub, dk), F32)

    r = lax.broadcasted_iota(jnp.int32, (cl, cl), 0)
    cc = lax.broadcasted_iota(jnp.int32, (cl, cl), 1)
    lower = jnp.where(cc <= r, 1.0, 0.0).astype(BF16)
    tmod = lax.broadcasted_iota(jnp.int32, (cl, dk), 0) % sub

    for hh in range(hb):
        cols = slice(hh * dk, (hh + 1) * dk)
        q = q_ref[:, cols].astype(F32)
        lg = lg_ref[:, cols]
        vb = v_ref[:, cols]
        v = vb.astype(F32)
        hi, mid, lo = _split3(lg)
        gcum = (_dot(lower, hi) + _dot(lower, mid)) + _dot(lower, lo)
        k = 1.0 - jnp.exp(lg)
        st = state_ref[hh]

        oacc_ref[...] = _dot_nt((q * jnp.exp(gcum)).astype(BF16), st.astype(BF16))

        m = cl // 2
        while m >= sub:
            for blk in range(cl // (2 * m)):
                r0 = blk * 2 * m
                mid_row = r0 + m
                gm = gcum[mid_row - 1:mid_row, :]
                qs = (q[mid_row:mid_row + m] * jnp.exp(gcum[mid_row:mid_row + m] - gm)).astype(BF16)
                ks = (k[r0:mid_row] * jnp.exp(gm - gcum[r0:mid_row])).astype(BF16)
                sc = _dot_nt(qs, ks)
                oacc_ref[mid_row:mid_row + m, :] += _dot(sc.astype(BF16), vb[r0:mid_row])
            m //= 2

        kpad_ref[sub:, :] = k
        gpad_ref[sub:, :] = gcum
        vpad_ref[sub:, :] = v
        diag = jnp.zeros((cl, dk), F32)
        for delta in range(sub):
            kd = kpad_ref[sub - delta:sub - delta + cl, :]
            gd = gpad_ref[sub - delta:sub - delta + cl, :]
            vd = vpad_ref[sub - delta:sub - delta + cl, :]
            ok = tmod >= delta
            a = q * kd * jnp.exp(jnp.where(ok, gcum - gd, -jnp.inf))
            diag = diag + jnp.sum(a, axis=-1, keepdims=True) * vd
        o = oacc_ref[...] + diag

        gl = gcum[cl - 1:cl, :]
        kdec = (k * jnp.exp(gl - gcum)).astype(BF16)
        state_ref[hh] = st * jnp.exp(gl) + _dot_tn(vb, kdec)

        ms = jnp.mean(o * o, axis=-1, keepdims=True)
        y = o * lax.rsqrt(ms + EPS) * gain_ref[...]
        o_ref[:, cols] = (y * gate_ref[:, cols].astype(F32)).astype(BF16)


def _hgrn_core(q, lg, v, gate, gain, bsz, seq, cl=128, hb=4):
    nc = seq // cl
    d = q.shape[1]
    blk = lambda b, g, c: (b * nc + c, g)
    spec = pl.BlockSpec((cl, hb * HEAD_DIM), blk)
    return pl.pallas_call(
        functools.partial(_hgrn_kernel, hb=hb),
        grid=(bsz, HGRN_HEADS // hb, nc),
        in_specs=[spec, spec, spec, spec, pl.BlockSpec((1, HEAD_DIM), lambda b, g, c: (0, 0))],
        out_specs=spec,
        out_shape=jax.ShapeDtypeStruct((bsz * seq, d), BF16),
        scratch_shapes=[
            pltpu.VMEM((hb, HEAD_DIM, HEAD_DIM), F32),
            pltpu.VMEM((cl, HEAD_DIM), F32),
            pltpu.VMEM((cl + HGRN_SUB, HEAD_DIM), F32),
            pltpu.VMEM((cl + HGRN_SUB, HEAD_DIM), F32),
            pltpu.VMEM((cl + HGRN_SUB, HEAD_DIM), F32),
        ],
        compiler_params=_params("parallel", "parallel", "arbitrary"),
        name="hgrn_core",
    )(q, lg, v, gate, gain.reshape(1, HEAD_DIM))


def _dsa_mixer(h, w_in, q_gain, kv_gain, bsz, seq):
    d = h.shape[1]
    nq = DSA_HEADS * DSA_LATENT
    w_q = w_in[:, :nq].astype(BF16)
    lat_w = w_in[:, nq:nq + DSA_LATENT]
    qi_w = w_in[:, nq + DSA_LATENT:nq + DSA_LATENT + IDX_HEADS * IDX_DIM]
    c0 = nq + DSA_LATENT + IDX_HEADS * IDX_DIM
    ki_w = w_in[:, c0:c0 + IDX_DIM]
    wi_w = w_in[:, c0 + IDX_DIM:c0 + IDX_DIM + IDX_HEADS]
    z = jnp.zeros((d, IDX_DIM), F32)
    w_s = jnp.concatenate(
        [lat_w, qi_w, ki_w, z, z, ki_w, wi_w, jnp.zeros((d, 128 - IDX_HEADS), F32)], axis=1).astype(BF16)
    q_gain_row = jnp.tile(q_gain, DSA_HEADS).reshape(1, nq)
    kv_gain_row = jnp.zeros((1, _DSA_S_N), F32).at[0, :DSA_LATENT].set(kv_gain)

    (q,) = _proj(h, w_q, 0, nq, _ep_dsa_q, [(1024, nq, BF16)], extras=(q_gain_row,))
    lat, qi, ke, ko, wi = _proj(
        h, w_s, 0, _DSA_S_N, _ep_dsa_small,
        [(DSA_LATENT, DSA_LATENT, BF16), (IDX_HEADS * IDX_DIM, IDX_HEADS * IDX_DIM, BF16),
         (2 * IDX_DIM, 2 * IDX_DIM, BF16), (2 * IDX_DIM, 2 * IDX_DIM, BF16), (IDX_HEADS, IDX_HEADS, F32)],
        extras=(kv_gain_row,), tn=_DSA_S_N)
    bias = _dsa_select(qi, ke, ko, wi, bsz, seq)
    return _dsa_attn(q, lat, bias, bsz, seq)


def _fox_mixer(h, w_in, b_f, q_gain, k_gain, bsz, seq):
    d = h.shape[1]
    w = w_in[:, :4 * d].astype(BF16)
    w_fl_t = w_in[:, 4 * d:].T.astype(BF16)
    qg = jnp.tile(q_gain, FOX_HEADS).reshape(1, d)
    kg = jnp.tile(k_gain, FOX_HEADS).reshape(1, d)
    (q,) = _proj(h, w, 0, d, _ep_fox_q, [(1024, d, BF16)], extras=(qg,))
    (k,) = _proj(h, w, d, d, _ep_fox_k, [(1024, d, BF16)], extras=(kg,))
    (v,) = _proj(h, w, 2 * d, d, _ep_cast, [(1024, d, BF16)])
    (gate,) = _proj(h, w, 3 * d, d, _ep_sigmoid, [(1024, d, BF16)])
    cum = _fox_cum(h, w_fl_t, b_f, bsz, seq)
    return _fox_attn(q, k, v, gate, cum, bsz, seq)


def _hgrn_mixer(h, w_in, lb, o_gain, bsz, seq):
    d = h.shape[1]
    w = w_in.astype(BF16)
    (q,) = _proj(h, w, 0, d, _ep_hgrn_q, [(1024, d, BF16)])
    (lg,) = _proj(h, w, d, d, _ep_hgrn_logf, [(1024, d, F32)], extras=(lb.reshape(1, d),))
    (v,) = _proj(h, w, 2 * d, d, _ep_cast, [(1024, d, BF16)])
    (gate,) = _proj(h, w, 3 * d, d, _ep_silu, [(1024, d, BF16)])
    return _hgrn_core(q, lg, v, gate, o_gain, bsz, seq)


def kernel(x, c, ada_w, ada_b, norm_mix_g, norm_ffn_g, dsa_w_in, dsa_q_norm, dsa_kv_norm, dsa_w_out, fox_w_in, fox_b_f, fox_q_norm, fox_k_norm, fox_w_out, hgrn_w_in, hgrn_lb, hgrn_o_norm, hgrn_w_out, ffn_w_in, ffn_conv_w, ffn_conv_b, ffn_w_out):
    bsz, seq, d = x.shape
    depth = ada_w.shape[0]
    f = ffn_conv_b.shape[1]

    mod = _ada_all(c, ada_w, ada_b)
    lb_soft = jax.nn.softmax(hgrn_lb.astype(F32), axis=0)
    lb_all = jnp.cumsum(lb_soft, axis=0) - lb_soft[0]
    conv_b = ffn_conv_b.reshape(depth, 1, f)

    x2 = x.reshape(bsz * seq, d)
    for i in range(depth):
        sh1, sc1, g1, sh2, sc2, g2 = [mod[i, :, j * d:(j + 1) * d] for j in range(6)]
        h = _norm_mod(x2, norm_mix_g[i], sh1, sc1, seq)
        kind, j = i % 3, i // 3
        if kind == 0:
            a = _dsa_mixer(h, dsa_w_in[j], dsa_q_norm[j], dsa_kv_norm[j], bsz, seq)
            w_out = dsa_w_out[j]
        elif kind == 1:
            a = _fox_mixer(h, fox_w_in[j], fox_b_f[j], fox_q_norm[j], fox_k_norm[j], bsz, seq)
            w_out = fox_w_out[j]
        else:
            a = _hgrn_mixer(h, hgrn_w_in[j], lb_all[i], hgrn_o_norm[j], bsz, seq)
            w_out = hgrn_w_out[j]
        x2 = _out_proj(a, w_out.astype(BF16), x2, g1, seq, tm=1024)
        h = _norm_mod(x2, norm_ffn_g[i], sh2, sc2, seq)
        a = _ffn_in(h, ffn_w_in[i].astype(BF16), ffn_conv_w, conv_b, i, seq)
        x2 = _out_proj(a, ffn_w_out[i].astype(BF16), x2, g2, seq, tm=512)
    return x2.reshape(bsz, seq, d)
```

```python
import functools

import jax
import jax.numpy as jnp
from jax import lax
from jax.experimental import pallas as pl
from jax.experimental.pallas import tpu as pltpu

F32 = jnp.float32
BF16 = jnp.bfloat16

EPS = 1e-6
CHUNK = 64
TOPK_MAX = 256

DSA_HEADS = 16
DSA_LATENT = 256
IDX_HEADS = 16
IDX_DIM = 64
FOX_HEADS = 16
HGRN_HEADS = 16
HEAD_DIM = 128
CONV_WIDTH = 3

NEG_BIG = -1e30
INT_MIN = -2 ** 31

V7X_VMEM_BYTES = 64 * 1024 * 1024
VMEM_LIMIT = 56 * 1024 * 1024


def _params(*sem):
    return pltpu.CompilerParams(dimension_semantics=sem, vmem_limit_bytes=VMEM_LIMIT)


def _dot(a, b):
    return jnp.dot(a, b, preferred_element_type=F32)


def _dot_nt(a, b):
    return lax.dot_general(a, b, (((1,), (1,)), ((), ())), preferred_element_type=F32)


def _dot_tn(a, b):
    return lax.dot_general(a, b, (((0,), (0,)), ((), ())), preferred_element_type=F32)


def _silu(x):
    return x * jax.nn.sigmoid(x)


def _ada_kernel(c_ref, w_ref, b_ref, o_ref):
    cond = _silu(c_ref[...]).astype(BF16)
    o_ref[0] = _dot(cond, w_ref[0].astype(BF16)) + b_ref[0]


def _ada_all(c, ada_w, ada_b, tn=1024):
    depth, d, n = ada_w.shape
    bsz = c.shape[0]
    rows = 8
    c_pad = jnp.zeros((rows, d), F32).at[:bsz].set(c)
    out = pl.pallas_call(
        _ada_kernel,
        grid=(depth, n // tn),
        in_specs=[
            pl.BlockSpec((rows, d), lambda l, j: (0, 0)),
            pl.BlockSpec((1, d, tn), lambda l, j: (l, 0, j)),
            pl.BlockSpec((1, 1, tn), lambda l, j: (l, 0, j)),
        ],
        out_specs=pl.BlockSpec((1, rows, tn), lambda l, j: (l, 0, j)),
        out_shape=jax.ShapeDtypeStruct((depth, rows, n), F32),
        compiler_params=_params("parallel", "parallel"),
        name="ada_mod",
    )(c_pad, ada_w, ada_b.reshape(depth, 1, n))
    return out[:, :bsz]


def _norm_mod_kernel(x_ref, g_ref, sh_ref, sc_ref, h_ref):
    x = x_ref[...]
    ms = jnp.mean(x * x, axis=-1, keepdims=True)
    y = x * lax.rsqrt(ms + EPS) * g_ref[...]
    h_ref[...] = (y * (1.0 + sc_ref[0]) + sh_ref[0]).astype(BF16)


def _norm_mod(x2, g, shift, scale, seq, tm=512):
    t, d = x2.shape
    per_seq = seq // tm
    bsz = shift.shape[0]
    vec = lambda m: (m // per_seq, 0, 0)
    return pl.pallas_call(
        _norm_mod_kernel,
        grid=(t // tm,),
        in_specs=[
            pl.BlockSpec((tm, d), lambda m: (m, 0)),
            pl.BlockSpec((1, d), lambda m: (0, 0)),
            pl.BlockSpec((1, 1, d), vec),
            pl.BlockSpec((1, 1, d), vec),
        ],
        out_specs=pl.BlockSpec((tm, d), lambda m: (m, 0)),
        out_shape=jax.ShapeDtypeStruct((t, d), BF16),
        compiler_params=_params("parallel"),
        name="norm_mod",
    )(x2, g.reshape(1, d), shift.reshape(bsz, 1, d), scale.reshape(bsz, 1, d))


def _proj_kernel(*refs, epilogue, n_extra):
    h_ref, w_ref = refs[0], refs[1]
    extras = refs[2:2 + n_extra]
    outs = refs[2 + n_extra:]
    acc = _dot(h_ref[...], w_ref[...])
    epilogue(acc, [e[...] for e in extras], outs)


def _proj(h, w, col0, n, epilogue, out_defs, extras=(), row_extras=(), tm=1024, tn=1024):
    t, k = h.shape
    tn = min(tn, n)
    tm = min(tm, t)
    assert n % tn == 0 and col0 % tn == 0 and t % tm == 0
    off = col0 // tn
    in_specs = [
        pl.BlockSpec((tm, k), lambda j, m: (m, 0)),
        pl.BlockSpec((k, tn), lambda j, m: (0, j + off)),
    ] + [pl.BlockSpec((1, tn), lambda j, m: (0, j)) for _ in extras
         ] + [pl.BlockSpec((tm, r.shape[1]), lambda j, m: (m, 0)) for r in row_extras]
    extras = tuple(extras) + tuple(row_extras)
    out_specs = [pl.BlockSpec((tm, wt), lambda j, m: (m, j)) for wt, _, _ in out_defs]
    out_shape = [jax.ShapeDtypeStruct((t, wtot), dt) for _, wtot, dt in out_defs]
    return pl.pallas_call(
        functools.partial(_proj_kernel, epilogue=epilogue, n_extra=len(extras)),
        grid=(n // tn, t // tm),
        in_specs=in_specs,
        out_specs=out_specs,
        out_shape=out_shape,
        compiler_params=_params("parallel", "parallel"),
        name="proj_" + epilogue.__name__.strip("_"),
    )(h, w, *extras)


def _proj_t_kernel(wt_ref, h_ref, o_ref):
    o_ref[...] = _dot_nt(wt_ref[...], h_ref[...]).astype(o_ref.dtype)


def _proj_t(h, wt, tm=1024, tn=1024):
    t, k = h.shape
    n = wt.shape[0]
    tm = min(tm, t)
    return pl.pallas_call(
        _proj_t_kernel,
        grid=(n // tn, t // tm),
        in_specs=[
            pl.BlockSpec((tn, k), lambda j, m: (j, 0)),
            pl.BlockSpec((tm, k), lambda j, m: (m, 0)),
        ],
        out_specs=pl.BlockSpec((tn, tm), lambda j, m: (j, m)),
        out_shape=jax.ShapeDtypeStruct((n, t), BF16),
        compiler_params=_params("parallel", "parallel"),
        name="proj_t",
    )(wt, h)


def _rms_heads(acc, gain_row, hd, scale, o_ref):
    for j in range(acc.shape[1] // hd):
        seg = acc[:, j * hd:(j + 1) * hd]
        ms = jnp.mean(seg * seg, axis=-1, keepdims=True)
        g = gain_row[:, j * hd:(j + 1) * hd]
        if scale != 1.0:
            g = g * scale
        o_ref[:, j * hd:(j + 1) * hd] = (seg * lax.rsqrt(ms + EPS) * g).astype(o_ref.dtype)


def _ep_dsa_q(acc, extras, outs):
    _rms_heads(acc, extras[0], DSA_LATENT, DSA_LATENT ** -0.5, outs[0])


FOX_AUG = 2 * HEAD_DIM
FOX_BIAS_LANES = 3


def _rms_heads_aug(acc, gain_row, scale, extra_fn, o_ref):
    for j in range(acc.shape[1] // HEAD_DIM):
        cols = slice(j * HEAD_DIM, (j + 1) * HEAD_DIM)
        seg = acc[:, cols]
        ms = jnp.mean(seg * seg, axis=-1, keepdims=True)
        y = seg * lax.rsqrt(ms + EPS) * (gain_row[:, cols] * scale)
        o_ref[:, j * FOX_AUG:j * FOX_AUG + HEAD_DIM] = y.astype(o_ref.dtype)
        o_ref[:, j * FOX_AUG + HEAD_DIM:(j + 1) * FOX_AUG] = extra_fn(j).astype(o_ref.dtype)


def _ep_fox_q(acc, extras, outs):
    lane = lax.broadcasted_iota(jnp.int32, (acc.shape[0], HEAD_DIM), 1)
    ones = jnp.where(lane < FOX_BIAS_LANES, 1.0, 0.0)
    _rms_heads_aug(acc, extras[0], HEAD_DIM ** -0.5, lambda j: ones, outs[0])


def _ep_fox_k(acc, extras, outs):
    gain_row, cum_t = extras
    heads = acc.shape[1] // HEAD_DIM
    first = pl.program_id(0) * heads
    r = lax.broadcasted_iota(jnp.int32, (FOX_HEADS, acc.shape[1]), 0)
    c = lax.broadcasted_iota(jnp.int32, (FOX_HEADS, acc.shape[1]), 1)
    head_col = jnp.where(r == first + c // HEAD_DIM, c % HEAD_DIM, -1)
    extra = jnp.zeros(acc.shape, F32)
    for i, term in enumerate(_split3(-cum_t)):
        extra = extra + _dot(term, jnp.where(head_col == i, 1.0, 0.0).astype(BF16))
    _rms_heads_aug(acc, gain_row, 1.0,
                   lambda j: extra[:, j * HEAD_DIM:(j + 1) * HEAD_DIM], outs[0])


def _ep_cast(acc, extras, outs):
    outs[0][...] = acc.astype(outs[0].dtype)


def _ep_sigmoid(acc, extras, outs):
    outs[0][...] = jax.nn.sigmoid(acc).astype(outs[0].dtype)


def _ep_silu(acc, extras, outs):
    outs[0][...] = _silu(acc).astype(outs[0].dtype)


def _ep_hgrn_q(acc, extras, outs):
    outs[0][...] = (_silu(acc) * (HEAD_DIM ** -0.5)).astype(outs[0].dtype)


def _ep_hgrn_logf(acc, extras, outs):
    lb = extras[0]
    outs[0][...] = jnp.log(lb + (1.0 - lb) * jax.nn.sigmoid(acc))


_DSA_S_LAT = 0
_DSA_S_QI = DSA_LATENT
_DSA_S_KE = _DSA_S_QI + IDX_HEADS * IDX_DIM
_DSA_S_KO = _DSA_S_KE + 2 * IDX_DIM
_DSA_S_WI = _DSA_S_KO + 2 * IDX_DIM
_DSA_S_N = _DSA_S_WI + 128


def _ep_dsa_small(acc, extras, outs):
    lat_ref, qi_ref, ke_ref, ko_ref, wi_ref = outs
    _rms_heads(acc[:, :DSA_LATENT], extras[0][:, :DSA_LATENT], DSA_LATENT, 1.0, lat_ref)
    qi_ref[...] = acc[:, _DSA_S_QI:_DSA_S_KE].astype(BF16)
    ke_ref[...] = acc[:, _DSA_S_KE:_DSA_S_KO].astype(BF16)
    ko_ref[...] = acc[:, _DSA_S_KO:_DSA_S_WI].astype(BF16)
    wi_ref[...] = acc[:, _DSA_S_WI:_DSA_S_WI + IDX_HEADS] * (IDX_HEADS ** -0.5 * IDX_DIM ** -0.5)


def _out_kernel(a_ref, w_ref, x_ref, g_ref, o_ref):
    o_ref[...] = x_ref[...] + g_ref[0] * _dot(a_ref[...], w_ref[...])


def _out_proj(a, w, x2, gate, seq, tm, tn=1024):
    t, k = a.shape
    d = w.shape[1]
    bsz = gate.shape[0]
    per_seq = seq // tm
    return pl.pallas_call(
        _out_kernel,
        grid=(d // tn, t // tm),
        in_specs=[
            pl.BlockSpec((tm, k), lambda j, m: (m, 0)),
            pl.BlockSpec((k, tn), lambda j, m: (0, j)),
            pl.BlockSpec((tm, tn), lambda j, m: (m, j)),
            pl.BlockSpec((1, 1, tn), lambda j, m: (m // per_seq, 0, j)),
        ],
        out_specs=pl.BlockSpec((tm, tn), lambda j, m: (m, j)),
        out_shape=jax.ShapeDtypeStruct((t, d), F32),
        compiler_params=_params("parallel", "parallel"),
        name="out_proj",
    )(a, w, x2, gate.reshape(bsz, 1, d))


def _ffn_in_kernel(h_ref, wu_ref, wg_ref, cw_ref, cb_ref, a_ref, carry_ref, *, per_seq):
    m = pl.program_id(1)
    h = h_ref[...]
    u = _dot(h, wu_ref[...])
    g = _dot(h, wg_ref[...])
    tm = g.shape[0]

    @pl.when(m % per_seq == 0)
    def _():
        carry_ref[...] = jnp.zeros_like(carry_ref)

    prev = carry_ref[...]
    p1 = prev[7:8, :]
    p2 = prev[6:7, :]
    row = lax.broadcasted_iota(jnp.int32, g.shape, 0)
    g1 = jnp.where(row == 0, p1, pltpu.roll(g, 1, axis=0))
    g2 = jnp.where(row == 0, p2, jnp.where(row == 1, p1, pltpu.roll(g, 2, axis=0)))
    carry_ref[...] = g[tm - 8:, :]
    cw = cw_ref[0]
    conv = cw[0:1] * g2 + cw[1:2] * g1 + cw[2:3] * g + cb_ref[0]
    a_ref[...] = (_silu(conv) * u).astype(BF16)


def _ffn_in(h, w, conv_w, conv_b, layer, seq, tm=1024, tn=512):
    t, k = h.shape
    f = w.shape[1] // 2
    nf = f // tn
    per_seq = seq // tm
    return pl.pallas_call(
        functools.partial(_ffn_in_kernel, per_seq=per_seq),
        grid=(nf, t // tm),
        in_specs=[
            pl.BlockSpec((tm, k), lambda j, m: (m, 0)),
            pl.BlockSpec((k, tn), lambda j, m: (0, j)),
            pl.BlockSpec((k, tn), lambda j, m: (0, j + nf)),
            pl.BlockSpec((1, CONV_WIDTH, tn), lambda j, m: (layer, 0, j)),
            pl.BlockSpec((1, 1, tn), lambda j, m: (layer, 0, j)),
        ],
        out_specs=pl.BlockSpec((tm, tn), lambda j, m: (m, j)),
        out_shape=jax.ShapeDtypeStruct((t, f), BF16),
        scratch_shapes=[pltpu.VMEM((8, tn), F32)],
        compiler_params=_params("parallel", "arbitrary"),
        name="ffn_in",
    )(h, w, w, conv_w, conv_b)


def _dsa_select_kernel(qi_ref, ke_ref, ko_ref, wit_ref, bias_ref, key_ref, *, topk, ts):
    i = pl.program_id(1)
    tq = qi_ref.shape[0]
    seq = ke_ref.shape[0]
    n_tiles = ((i + 1) * tq + ts - 1) // ts
    qi = qi_ref[...]
    lhs = jnp.concatenate([qi[:, 128 * j:128 * (j + 1)] for j in range(IDX_HEADS // 2)], axis=0)
    wit = wit_ref[...]
    tpos = i * tq + lax.broadcasted_iota(jnp.int32, (ts, tq), 1)
    first_hidden = (tpos // CHUNK + 1) * CHUNK
    row = lax.broadcasted_iota(jnp.int32, (ts, tq), 0)

    def score_tile(t, carry):
        k0 = pl.multiple_of(t * ts, ts)
        sce = _dot_nt(ke_ref[pl.ds(k0, ts), :], lhs)
        sco = _dot_nt(ko_ref[pl.ds(k0, ts), :], lhs)
        score = jnp.zeros((ts, tq), F32)
        for j in range(IDX_HEADS // 2):
            cols = slice(j * tq, (j + 1) * tq)
            score = score + wit[2 * j:2 * j + 1] * jnp.maximum(sce[:, cols], 0.0)
            score = score + wit[2 * j + 1:2 * j + 2] * jnp.maximum(sco[:, cols], 0.0)
        score = score + 0.0
        bits = pltpu.bitcast(score, jnp.int32)
        key = jnp.where(bits < 0, bits ^ jnp.int32(0x7FFFFFFF), bits)
        key_ref[pl.ds(k0, ts), :] = jnp.where(k0 + row < first_hidden, key, INT_MIN)
        return carry

    lax.fori_loop(0, n_tiles, score_tile, 0)
    kf = jnp.float32(topk)

    def count_ge(cand):
        def body(t, cnt):
            k0 = pl.multiple_of(t * ts, ts)
            hit = jnp.where(key_ref[pl.ds(k0, ts), :] >= cand, 1.0, 0.0)
            return cnt + jnp.sum(hit, axis=0, keepdims=True)
        return lax.fori_loop(0, n_tiles, body, jnp.zeros((1, tq), F32))

    thr = jnp.where(count_ge(jnp.zeros((1, tq), jnp.int32)) >= kf, 0, INT_MIN).astype(jnp.int32)

    def bit_step(b, thr):
        cand = thr + lax.shift_left(jnp.int32(1), jnp.int32(30) - b)
        return jnp.where(count_ge(cand) >= kf, cand, thr)

    thr = lax.fori_loop(0, 31, bit_step, thr)

    def write_tile(t, carry):
        k0 = pl.multiple_of(t * ts, ts)
        key = key_ref[pl.ds(k0, ts), :]
        keep = jnp.where(key >= thr, jnp.where(key > INT_MIN, 0.0, NEG_BIG), NEG_BIG)
        bias_ref[0, pl.ds(k0, ts), :] = keep.astype(BF16)
        return carry

    def fill_tile(t, carry):
        k0 = pl.multiple_of(t * ts, ts)
        bias_ref[0, pl.ds(k0, ts), :] = jnp.full((ts, tq), NEG_BIG, BF16)
        return carry

    lax.fori_loop(0, n_tiles, write_tile, 0)
    lax.fori_loop(n_tiles, seq // ts, fill_tile, 0)


def _dsa_select(qi, ke, ko, wit, bsz, seq, tq=256, ts=256):
    nq = seq // tq
    topk = min(TOPK_MAX, seq // 4)
    return pl.pallas_call(
        functools.partial(_dsa_select_kernel, topk=topk, ts=ts),
        grid=(bsz, nq),
        in_specs=[
            pl.BlockSpec((tq, IDX_HEADS * IDX_DIM), lambda b, i: (b * nq + i, 0)),
            pl.BlockSpec((seq, 2 * IDX_DIM), lambda b, i: (b, 0)),
            pl.BlockSpec((seq, 2 * IDX_DIM), lambda b, i: (b, 0)),
            pl.BlockSpec((IDX_HEADS, tq), lambda b, i: (0, b * nq + i)),
        ],
        out_specs=pl.BlockSpec((1, seq, tq), lambda b, i: (b, 0, i)),
        out_shape=jax.ShapeDtypeStruct((bsz, seq, seq), BF16),
        scratch_shapes=[pltpu.VMEM((seq, tq), jnp.int32)],
        compiler_params=_params("parallel", "parallel"),
        name="dsa_select",
    )(qi, ke, ko, wit)


def _dsa_attn_kernel(q_ref, lat_ref, latt_ref, bias_ref, o_ref, m_ref, l_ref, acc_ref, *, hg, ng, tk):
    i = pl.program_id(1)
    tq = q_ref.shape[0]
    r = DSA_LATENT
    n_tiles = ((i + 1) * tq + tk - 1) // tk
    cols = hg * tq
    tpos = i * tq + lax.broadcasted_iota(jnp.int32, (tk, tq), 1)
    row = lax.broadcasted_iota(jnp.int32, (tk, tq), 0)
    for g0 in range(0, DSA_HEADS // hg, ng):
        groups = [range((g0 + n) * hg, (g0 + n + 1) * hg) for n in range(ng)]
        q4 = [jnp.concatenate([q_ref[:, h * r:(h + 1) * r] for h in heads], axis=0)
              for heads in groups]
        m_ref[...] = jnp.full(m_ref.shape, NEG_BIG, F32)
        l_ref[...] = jnp.zeros(l_ref.shape, F32)
        acc_ref[...] = jnp.zeros(acc_ref.shape, F32)

        def body(j, carry):
            k0 = pl.multiple_of(j * tk, tk)
            lat = lat_ref[pl.ds(k0, tk), :]
            latt = latt_ref[:, pl.ds(k0, tk)]
            dist = jnp.abs(tpos - (k0 + row)).astype(F32)
            bias = bias_ref[0, pl.ds(k0, tk), :].astype(F32)
            qk = [_dot_nt(lat, q4[n]) for n in range(ng)]
            ps, alphas = [], []
            for n, heads in enumerate(groups):
                base = jnp.concatenate(
                    [bias - (2.0 ** (-8.0 * (h + 1) / DSA_HEADS)) * dist for h in heads], axis=1)
                s = qk[n] + base
                m_old = m_ref[n]
                m_new = jnp.maximum(m_old, jnp.max(s, axis=0, keepdims=True))
                alpha = jnp.exp(m_old - m_new)
                p = jnp.exp(s - m_new)
                l_ref[n] = alpha * l_ref[n] + jnp.sum(p, axis=0, keepdims=True)
                m_ref[n] = m_new
                ps.append(p.astype(BF16))
                alphas.append(alpha)
            pvs = [_dot(latt, ps[n]) for n in range(ng)]
            for n in range(ng):
                acc_ref[n] = alphas[n] * acc_ref[n] + pvs[n]
            return carry

        lax.fori_loop(0, n_tiles, body, 0)
        for n, heads in enumerate(groups):
            out = (acc_ref[n] / l_ref[n]).T
            for idx, h in enumerate(heads):
                o_ref[:, h * r:(h + 1) * r] = out[idx * tq:(idx + 1) * tq].astype(BF16)


def _dsa_attn(q, lat, latt, bias, bsz, seq, tq=128, hg=4, ng=2, tk=256):
    nq = seq // tq
    width = DSA_HEADS * DSA_LATENT
    return pl.pallas_call(
        functools.partial(_dsa_attn_kernel, hg=hg, ng=ng, tk=tk),
        grid=(bsz, nq),
        in_specs=[
            pl.BlockSpec((tq, width), lambda b, i: (b * nq + i, 0)),
            pl.BlockSpec((seq, DSA_LATENT), lambda b, i: (b, 0)),
            pl.BlockSpec((DSA_LATENT, seq), lambda b, i: (0, b)),
            pl.BlockSpec((1, seq, tq), lambda b, i: (b, 0, i)),
        ],
        out_specs=pl.BlockSpec((tq, width), lambda b, i: (b * nq + i, 0)),
        out_shape=jax.ShapeDtypeStruct((bsz * seq, width), BF16),
        scratch_shapes=[
            pltpu.VMEM((ng, 1, hg * tq), F32),
            pltpu.VMEM((ng, 1, hg * tq), F32),
            pltpu.VMEM((ng, DSA_LATENT, hg * tq), F32),
        ],
        compiler_params=_params("parallel", "parallel"),
        name="dsa_attn",
    )(q, lat, latt, bias)


def _split3(x):
    hi = x.astype(BF16)
    r1 = x - hi.astype(F32)
    mid = r1.astype(BF16)
    lo = (r1 - mid.astype(F32)).astype(BF16)
    return hi, mid, lo


def _fox_cum_kernel(h_ref, wt_ref, b_ref, cum_ref, *, blk):
    fl = _dot_nt(wt_ref[...], h_ref[...]) + b_ref[...]
    logf = jax.nn.log_sigmoid(fl)
    seq = logf.shape[1]
    r = lax.broadcasted_iota(jnp.int32, (blk, blk), 0)
    c = lax.broadcasted_iota(jnp.int32, (blk, blk), 1)
    upper = jnp.where(r <= c, 1.0, 0.0).astype(BF16)
    carry = jnp.zeros((logf.shape[0], 1), F32)
    for j in range(seq // blk):
        hi, mid, lo = _split3(logf[:, j * blk:(j + 1) * blk])
        cs = (_dot(hi, upper) + _dot(mid, upper)) + _dot(lo, upper) + carry
        cum_ref[0, :, j * blk:(j + 1) * blk] = cs
        carry = cs[:, blk - 1:blk]


def _fox_cum(h, w_fl_t, b_f, bsz, seq, blk=256):
    d = h.shape[1]
    return pl.pallas_call(
        functools.partial(_fox_cum_kernel, blk=blk),
        grid=(bsz,),
        in_specs=[
            pl.BlockSpec((seq, d), lambda b: (b, 0)),
            pl.BlockSpec((FOX_HEADS, d), lambda b: (0, 0)),
            pl.BlockSpec((FOX_HEADS, 1), lambda b: (0, 0)),
        ],
        out_specs=pl.BlockSpec((1, FOX_HEADS, seq), lambda b: (b, 0, 0)),
        out_shape=jax.ShapeDtypeStruct((bsz, FOX_HEADS, seq), F32),
        compiler_params=_params("parallel"),
        name="fox_cum",
    )(h, w_fl_t, b_f.reshape(FOX_HEADS, 1))


def _fox_attn_kernel(q_ref, k_ref, vt_ref, g_ref, o_ref, m_ref, l_ref, acc_ref):
    i = pl.program_id(2)
    tq = q_ref.shape[0]
    hb = g_ref.shape[1] // HEAD_DIM
    qs = [q_ref[:, h * FOX_AUG:(h + 1) * FOX_AUG] for h in range(hb)]
    m_ref[...] = jnp.full(m_ref.shape, NEG_BIG, F32)
    l_ref[...] = jnp.zeros(l_ref.shape, F32)
    acc_ref[...] = jnp.zeros(acc_ref.shape, F32)

    def tile(j, diagonal):
        k0 = pl.multiple_of(j * tq, tq)
        ss = [_dot_nt(k_ref[pl.ds(k0, tq), h * FOX_AUG:(h + 1) * FOX_AUG], qs[h])
              for h in range(hb)]
        ps, alphas = [], []
        for h in range(hb):
            s = ss[h]
            if diagonal:
                key = lax.broadcasted_iota(jnp.int32, (tq, tq), 0)
                qry = lax.broadcasted_iota(jnp.int32, (tq, tq), 1)
                s = jnp.where(key <= qry, s, NEG_BIG)
            m_old = m_ref[h]
            m_new = jnp.maximum(m_old, jnp.max(s, axis=0, keepdims=True))
            alpha = jnp.exp(m_old - m_new)
            p = jnp.exp(s - m_new)
            l_ref[h] = alpha * l_ref[h] + jnp.sum(p, axis=0, keepdims=True)
            m_ref[h] = m_new
            ps.append(p.astype(BF16))
            alphas.append(alpha)
        pvs = [_dot(vt_ref[h * HEAD_DIM:(h + 1) * HEAD_DIM, pl.ds(k0, tq)], ps[h])
               for h in range(hb)]
        for h in range(hb):
            acc_ref[h] = alphas[h] * acc_ref[h] + pvs[h]

    def body(j, carry):
        tile(j, False)
        return carry

    lax.fori_loop(0, i, body, 0)
    tile(i, True)
    for h in range(hb):
        cols = slice(h * HEAD_DIM, (h + 1) * HEAD_DIM)
        o = (acc_ref[h] / l_ref[h]).T
        o_ref[:, cols] = (o * g_ref[:, cols].astype(F32)).astype(BF16)


def _fox_attn(q, k, vt, gate, bsz, seq, tq=256, hb=4):
    nq = seq // tq
    d = gate.shape[1]
    qmap = lambda b, h, i: (b * nq + i, h)
    return pl.pallas_call(
        _fox_attn_kernel,
        grid=(bsz, FOX_HEADS // hb, nq),
        in_specs=[
            pl.BlockSpec((tq, hb * FOX_AUG), qmap),
            pl.BlockSpec((seq, hb * FOX_AUG), lambda b, h, i: (b, h)),
            pl.BlockSpec((hb * HEAD_DIM, seq), lambda b, h, i: (h, b)),
            pl.BlockSpec((tq, hb * HEAD_DIM), qmap),
        ],
        out_specs=pl.BlockSpec((tq, hb * HEAD_DIM), qmap),
        out_shape=jax.ShapeDtypeStruct((bsz * seq, d), BF16),
        scratch_shapes=[
            pltpu.VMEM((hb, 1, tq), F32),
            pltpu.VMEM((hb, 1, tq), F32),
            pltpu.VMEM((hb, HEAD_DIM, tq), F32),
        ],
        compiler_params=_params("parallel", "parallel", "parallel"),
        name="fox_attn",
    )(q, k, vt, gate)


HGRN_SUB = 16


def _hgrn_kernel(q_ref, lg_ref, v_ref, gate_ref, gain_ref, o_ref,
                 state_ref, oacc_ref, kpad_ref, gpad_ref, vpad_ref, *, hb):
    c = pl.program_id(2)
    cl = q_ref.shape[0]
    dk = HEAD_DIM
    sub = HGRN_SUB

    @pl.when(c == 0)
    def _():
        state_ref[...] = jnp.zeros_like(state_ref)
        kpad_ref[0:sub, :] = jnp.zeros((sub, dk), F32)
        gpad_ref[0:sub, :] = jnp.zeros((sub, dk), F32)
        vpad_ref[0:sub, :] = jnp.zeros((sub, dk), F32)

    r = lax.broadcasted_iota(jnp.int32, (cl, cl), 0)
    cc = lax.broadcasted_iota(jnp.int32, (cl, cl), 1)
    lower = jnp.where(cc <= r, 1.0, 0.0).astype(BF16)
    tmod = lax.broadcasted_iota(jnp.int32, (cl, dk), 0) % sub

    for hh in range(hb):
        cols = slice(hh * dk, (hh + 1) * dk)
        q = q_ref[:, cols].astype(F32)
        lg = lg_ref[:, cols]
        vb = v_ref[:, cols]
        v = vb.astype(F32)
        hi, mid, lo = _split3(lg)
        gcum = (_dot(lower, hi) + _dot(lower, mid)) + _dot(lower, lo)
        k = 1.0 - jnp.exp(lg)
        st = state_ref[hh]

        oacc_ref[...] = _dot_nt((q * jnp.exp(gcum)).astype(BF16), st.astype(BF16))

        m = cl // 2
        while m >= sub:
            for blk in range(cl // (2 * m)):
                r0 = blk * 2 * m
                mid_row = r0 + m
                gm = gcum[mid_row - 1:mid_row, :]
                qs = (q[mid_row:mid_row + m] * jnp.exp(gcum[mid_row:mid_row + m] - gm)).astype(BF16)
                ks = (k[r0:mid_row] * jnp.exp(gm - gcum[r0:mid_row])).astype(BF16)
                sc = _dot_nt(qs, ks)
                oacc_ref[mid_row:mid_row + m, :] += _dot(sc.astype(BF16), vb[r0:mid_row])
            m //= 2

        kpad_ref[sub:, :] = k
        gpad_ref[sub:, :] = gcum
        vpad_ref[sub:, :] = v
        diag = jnp.zeros((cl, dk), F32)
        for delta in range(sub):
            kd = kpad_ref[sub - delta:sub - delta + cl, :]
            gd = gpad_ref[sub - delta:sub - delta + cl, :]
            vd = vpad_ref[sub - delta:sub - delta + cl, :]
            ok = tmod >= delta
            a = q * kd * jnp.exp(jnp.where(ok, gcum - gd, -jnp.inf))
            diag = diag + jnp.sum(a, axis=-1, keepdims=True) * vd
        o = oacc_ref[...] + diag

        gl = gcum[cl - 1:cl, :]
        kdec = (k * jnp.exp(gl - gcum)).astype(BF16)
        state_ref[hh] = st * jnp.exp(gl) + _dot_tn(vb, kdec)

        ms = jnp.mean(o * o, axis=-1, keepdims=True)
        y = o * lax.rsqrt(ms + EPS) * gain_ref[...]
        o_ref[:, cols] = (y * gate_ref[:, cols].astype(F32)).astype(BF16)


def _hgrn_core(q, lg, v, gate, gain, bsz, seq, cl=128, hb=4):
    nc = seq // cl
    d = q.shape[1]
    blk = lambda b, g, c: (b * nc + c, g)
    spec = pl.BlockSpec((cl, hb * HEAD_DIM), blk)
    return pl.pallas_call(
        functools.partial(_hgrn_kernel, hb=hb),
        grid=(bsz, HGRN_HEADS // hb, nc),
        in_specs=[spec, spec, spec, spec, pl.BlockSpec((1, HEAD_DIM), lambda b, g, c: (0, 0))],
        out_specs=spec,
        out_shape=jax.ShapeDtypeStruct((bsz * seq, d), BF16),
        scratch_shapes=[
            pltpu.VMEM((hb, HEAD_DIM, HEAD_DIM), F32),
            pltpu.VMEM((cl, HEAD_DIM), F32),
            pltpu.VMEM((cl + HGRN_SUB, HEAD_DIM), F32),
            pltpu.VMEM((cl + HGRN_SUB, HEAD_DIM), F32),
            pltpu.VMEM((cl + HGRN_SUB, HEAD_DIM), F32),
        ],
        compiler_params=_params("parallel", "parallel", "arbitrary"),
        name="hgrn_core",
    )(q, lg, v, gate, gain.reshape(1, HEAD_DIM))


def _dsa_mixer(h, w_in, q_gain, kv_gain, bsz, seq):
    d = h.shape[1]
    nq = DSA_HEADS * DSA_LATENT
    w_q = w_in[:, :nq].astype(BF16)
    lat_w = w_in[:, nq:nq + DSA_LATENT]
    qi_w = w_in[:, nq + DSA_LATENT:nq + DSA_LATENT + IDX_HEADS * IDX_DIM]
    c0 = nq + DSA_LATENT + IDX_HEADS * IDX_DIM
    ki_w = w_in[:, c0:c0 + IDX_DIM]
    wi_w = w_in[:, c0 + IDX_DIM:c0 + IDX_DIM + IDX_HEADS]
    z = jnp.zeros((d, IDX_DIM), F32)
    w_s = jnp.concatenate(
        [lat_w, qi_w, ki_w, z, z, ki_w, wi_w, jnp.zeros((d, 128 - IDX_HEADS), F32)], axis=1).astype(BF16)
    q_gain_row = jnp.tile(q_gain, DSA_HEADS).reshape(1, nq)
    kv_gain_row = jnp.zeros((1, _DSA_S_N), F32).at[0, :DSA_LATENT].set(kv_gain)

    (q,) = _proj(h, w_q, 0, nq, _ep_dsa_q, [(1024, nq, BF16)], extras=(q_gain_row,))
    lat, qi, ke, ko, wi = _proj(
        h, w_s, 0, _DSA_S_N, _ep_dsa_small,
        [(DSA_LATENT, DSA_LATENT, BF16), (IDX_HEADS * IDX_DIM, IDX_HEADS * IDX_DIM, BF16),
         (2 * IDX_DIM, 2 * IDX_DIM, BF16), (2 * IDX_DIM, 2 * IDX_DIM, BF16), (IDX_HEADS, IDX_HEADS, F32)],
        extras=(kv_gain_row,), tn=_DSA_S_N)
    bias = _dsa_select(qi, ke, ko, wi.T, bsz, seq)
    return _dsa_attn(q, lat, lat.T, bias, bsz, seq)


def _fox_mixer(h, w_in, b_f, q_gain, k_gain, bsz, seq):
    d = h.shape[1]
    w = w_in[:, :4 * d].astype(BF16)
    w_fl_t = w_in[:, 4 * d:].T.astype(BF16)
    qg = jnp.tile(q_gain, FOX_HEADS).reshape(1, d)
    kg = jnp.tile(k_gain, FOX_HEADS).reshape(1, d)
    cum = _fox_cum(h, w_fl_t, b_f, bsz, seq)
    cum_t = cum.transpose(0, 2, 1).reshape(bsz * seq, FOX_HEADS)
    aug = [(1024 // HEAD_DIM * FOX_AUG, FOX_HEADS * FOX_AUG, BF16)]
    (q,) = _proj(h, w, 0, d, _ep_fox_q, aug, extras=(qg,))
    (k,) = _proj(h, w, d, d, _ep_fox_k, aug, extras=(kg,), row_extras=(cum_t,))
    vt = _proj_t(h, w_in[:, 2 * d:3 * d].T.astype(BF16))
    (gate,) = _proj(h, w, 3 * d, d, _ep_sigmoid, [(1024, d, BF16)])
    return _fox_attn(q, k, vt, gate, bsz, seq)


def _hgrn_mixer(h, w_in, lb, o_gain, bsz, seq):
    d = h.shape[1]
    w = w_in.astype(BF16)
    (q,) = _proj(h, w, 0, d, _ep_hgrn_q, [(1024, d, BF16)])
    (lg,) = _proj(h, w, d, d, _ep_hgrn_logf, [(1024, d, F32)], extras=(lb.reshape(1, d),))
    (v,) = _proj(h, w, 2 * d, d, _ep_cast, [(1024, d, BF16)])
    (gate,) = _proj(h, w, 3 * d, d, _ep_silu, [(1024, d, BF16)])
    return _hgrn_core(q, lg, v, gate, o_gain, bsz, seq)


def kernel(x, c, ada_w, ada_b, norm_mix_g, norm_ffn_g, dsa_w_in, dsa_q_norm, dsa_kv_norm, dsa_w_out, fox_w_in, fox_b_f, fox_q_norm, fox_k_norm, fox_w_out, hgrn_w_in, hgrn_lb, hgrn_o_norm, hgrn_w_out, ffn_w_in, ffn_conv_w, ffn_conv_b, ffn_w_out):
    bsz, seq, d = x.shape
    depth = ada_w.shape[0]
    f = ffn_conv_b.shape[1]

    mod = _ada_all(c, ada_w, ada_b)
    lb_soft = jax.nn.softmax(hgrn_lb.astype(F32), axis=0)
    lb_all = jnp.cumsum(lb_soft, axis=0) - lb_soft[0]
    conv_b = ffn_conv_b.reshape(depth, 1, f)

    x2 = x.reshape(bsz * seq, d)
    for i in range(depth):
        sh1, sc1, g1, sh2, sc2, g2 = [mod[i, :, j * d:(j + 1) * d] for j in range(6)]
        h = _norm_mod(x2, norm_mix_g[i], sh1, sc1, seq)
        kind, j = i % 3, i // 3
        if kind == 0:
            a = _dsa_mixer(h, dsa_w_in[j], dsa_q_norm[j], dsa_kv_norm[j], bsz, seq)
            w_out = dsa_w_out[j]
        elif kind == 1:
            a = _fox_mixer(h, fox_w_in[j], fox_b_f[j], fox_q_norm[j], fox_k_norm[j], bsz, seq)
            w_out = fox_w_out[j]
        else:
            a = _hgrn_mixer(h, hgrn_w_in[j], lb_all[i], hgrn_o_norm[j], bsz, seq)
            w_out = hgrn_w_out[j]
        x2 = _out_proj(a, w_out.astype(BF16), x2, g1, seq, tm=1024)
        h = _norm_mod(x2, norm_ffn_g[i], sh2, sc2, seq)
        a = _ffn_in(h, ffn_w_in[i].astype(BF16), ffn_conv_w, conv_b, i, seq)
        x2 = _out_proj(a, ffn_w_out[i].astype(BF16), x2, g2, seq, tm=512)
    return x2.reshape(bsz, seq, d)
```

```python
import functools

import jax
import jax.numpy as jnp
from jax import lax
from jax.experimental import pallas as pl
from jax.experimental.pallas import tpu as pltpu

F32 = jnp.float32
BF16 = jnp.bfloat16

EPS = 1e-6
CHUNK = 64
TOPK_MAX = 256

DSA_HEADS = 16
DSA_LATENT = 256
IDX_HEADS = 16
IDX_DIM = 64
FOX_HEADS = 16
HGRN_HEADS = 16
HEAD_DIM = 128
CONV_WIDTH = 3

NEG_BIG = -1e30
INT_MIN = -2 ** 31

V7X_VMEM_BYTES = 64 * 1024 * 1024
VMEM_LIMIT = 56 * 1024 * 1024


def _params(*sem):
    return pltpu.CompilerParams(dimension_semantics=sem, vmem_limit_bytes=VMEM_LIMIT)


def _dot(a, b):
    return jnp.dot(a, b, preferred_element_type=F32)


def _dot_nt(a, b):
    return lax.dot_general(a, b, (((1,), (1,)), ((), ())), preferred_element_type=F32)


def _dot_tn(a, b):
    return lax.dot_general(a, b, (((0,), (0,)), ((), ())), preferred_element_type=F32)


def _silu(x):
    return x * jax.nn.sigmoid(x)


def _ada_kernel(c_ref, w_ref, b_ref, o_ref):
    cond = _silu(c_ref[...]).astype(BF16)
    o_ref[0] = _dot(cond, w_ref[0].astype(BF16)) + b_ref[0]


def _ada_all(c, ada_w, ada_b, tn=1024):
    depth, d, n = ada_w.shape
    bsz = c.shape[0]
    rows = 8
    c_pad = jnp.zeros((rows, d), F32).at[:bsz].set(c)
    out = pl.pallas_call(
        _ada_kernel,
        grid=(depth, n // tn),
        in_specs=[
            pl.BlockSpec((rows, d), lambda l, j: (0, 0)),
            pl.BlockSpec((1, d, tn), lambda l, j: (l, 0, j)),
            pl.BlockSpec((1, 1, tn), lambda l, j: (l, 0, j)),
        ],
        out_specs=pl.BlockSpec((1, rows, tn), lambda l, j: (l, 0, j)),
        out_shape=jax.ShapeDtypeStruct((depth, rows, n), F32),
        compiler_params=_params("parallel", "parallel"),
        name="ada_mod",
    )(c_pad, ada_w, ada_b.reshape(depth, 1, n))
    return out[:, :bsz]


def _norm_mod_kernel(x_ref, g_ref, sh_ref, sc_ref, h_ref):
    x = x_ref[...]
    ms = jnp.mean(x * x, axis=-1, keepdims=True)
    y = x * lax.rsqrt(ms + EPS) * g_ref[...]
    h_ref[...] = (y * (1.0 + sc_ref[0]) + sh_ref[0]).astype(BF16)


def _norm_mod(x2, g, shift, scale, seq, tm=512):
    t, d = x2.shape
    per_seq = seq // tm
    bsz = shift.shape[0]
    vec = lambda m: (m // per_seq, 0, 0)
    return pl.pallas_call(
        _norm_mod_kernel,
        grid=(t // tm,),
        in_specs=[
            pl.BlockSpec((tm, d), lambda m: (m, 0)),
            pl.BlockSpec((1, d), lambda m: (0, 0)),
            pl.BlockSpec((1, 1, d), vec),
            pl.BlockSpec((1, 1, d), vec),
        ],
        out_specs=pl.BlockSpec((tm, d), lambda m: (m, 0)),
        out_shape=jax.ShapeDtypeStruct((t, d), BF16),
        compiler_params=_params("parallel"),
        name="norm_mod",
    )(x2, g.reshape(1, d), shift.reshape(bsz, 1, d), scale.reshape(bsz, 1, d))


def _proj_kernel(*refs, epilogue, n_extra, cast_w):
    h_ref, w_ref = refs[0], refs[1]
    extras = refs[2:2 + n_extra]
    if cast_w:
        outs, wb_ref = refs[2 + n_extra:-1], refs[-1]

        @pl.when(pl.program_id(1) == 0)
        def _():
            wb_ref[...] = w_ref[...].astype(BF16)

        w = wb_ref[...]
    else:
        outs = refs[2 + n_extra:]
        w = w_ref[...]
    acc = _dot(h_ref[...], w)
    epilogue(acc, [e[...] for e in extras], outs)


def _proj(h, w, col0, n, epilogue, out_defs, extras=(), row_extras=(), tm=1024, tn=1024):
    t, k = h.shape
    tn = min(tn, n)
    tm = min(tm, t)
    assert n % tn == 0 and col0 % tn == 0 and t % tm == 0
    off = col0 // tn
    in_specs = [
        pl.BlockSpec((tm, k), lambda j, m: (m, 0)),
        pl.BlockSpec((k, tn), lambda j, m: (0, j + off)),
    ] + [pl.BlockSpec((1, tn), lambda j, m: (0, j)) for _ in extras
         ] + [pl.BlockSpec((tm, r.shape[1]), lambda j, m: (m, 0)) for r in row_extras]
    extras = tuple(extras) + tuple(row_extras)
    out_specs = [pl.BlockSpec((tm, wt), lambda j, m: (m, j)) for wt, _, _ in out_defs]
    out_shape = [jax.ShapeDtypeStruct((t, wtot), dt) for _, wtot, dt in out_defs]
    cast_w = w.dtype != BF16
    return pl.pallas_call(
        functools.partial(_proj_kernel, epilogue=epilogue, n_extra=len(extras), cast_w=cast_w),
        grid=(n // tn, t // tm),
        in_specs=in_specs,
        out_specs=out_specs,
        out_shape=out_shape,
        scratch_shapes=[pltpu.VMEM((k, tn), BF16)] if cast_w else [],
        compiler_params=_params("parallel", "arbitrary"),
        name="proj_" + epilogue.__name__.strip("_"),
    )(h, w, *extras)


def _proj_t_kernel(wt_ref, h_ref, o_ref):
    o_ref[...] = _dot_nt(wt_ref[...], h_ref[...]).astype(o_ref.dtype)


def _proj_t(h, wt, tm=1024, tn=1024):
    t, k = h.shape
    n = wt.shape[0]
    tm = min(tm, t)
    return pl.pallas_call(
        _proj_t_kernel,
        grid=(n // tn, t // tm),
        in_specs=[
            pl.BlockSpec((tn, k), lambda j, m: (j, 0)),
            pl.BlockSpec((tm, k), lambda j, m: (m, 0)),
        ],
        out_specs=pl.BlockSpec((tn, tm), lambda j, m: (j, m)),
        out_shape=jax.ShapeDtypeStruct((n, t), BF16),
        compiler_params=_params("parallel", "parallel"),
        name="proj_t",
    )(wt, h)


def _rms_heads(acc, gain_row, hd, scale, o_ref):
    for j in range(acc.shape[1] // hd):
        seg = acc[:, j * hd:(j + 1) * hd]
        ms = jnp.mean(seg * seg, axis=-1, keepdims=True)
        g = gain_row[:, j * hd:(j + 1) * hd]
        if scale != 1.0:
            g = g * scale
        o_ref[:, j * hd:(j + 1) * hd] = (seg * lax.rsqrt(ms + EPS) * g).astype(o_ref.dtype)


def _ep_dsa_q(acc, extras, outs):
    _rms_heads(acc, extras[0], DSA_LATENT, DSA_LATENT ** -0.5, outs[0])


FOX_AUG = 2 * HEAD_DIM
FOX_BIAS_LANES = 3


def _rms_heads_aug(acc, gain_row, scale, extra_fn, o_ref):
    for j in range(acc.shape[1] // HEAD_DIM):
        cols = slice(j * HEAD_DIM, (j + 1) * HEAD_DIM)
        seg = acc[:, cols]
        ms = jnp.mean(seg * seg, axis=-1, keepdims=True)
        y = seg * lax.rsqrt(ms + EPS) * (gain_row[:, cols] * scale)
        o_ref[:, j * FOX_AUG:j * FOX_AUG + HEAD_DIM] = y.astype(o_ref.dtype)
        o_ref[:, j * FOX_AUG + HEAD_DIM:(j + 1) * FOX_AUG] = extra_fn(j).astype(o_ref.dtype)


def _ep_fox_q(acc, extras, outs):
    lane = lax.broadcasted_iota(jnp.int32, (acc.shape[0], HEAD_DIM), 1)
    ones = jnp.where(lane < FOX_BIAS_LANES, 1.0, 0.0)
    _rms_heads_aug(acc, extras[0], HEAD_DIM ** -0.5, lambda j: ones, outs[0])


def _ep_fox_k(acc, extras, outs):
    gain_row, cum_t = extras
    heads = acc.shape[1] // HEAD_DIM
    first = pl.program_id(0) * heads
    r = lax.broadcasted_iota(jnp.int32, (FOX_HEADS, acc.shape[1]), 0)
    c = lax.broadcasted_iota(jnp.int32, (FOX_HEADS, acc.shape[1]), 1)
    head_col = jnp.where(r == first + c // HEAD_DIM, c % HEAD_DIM, -1)
    extra = jnp.zeros(acc.shape, F32)
    for i, term in enumerate(_split3(-cum_t)):
        extra = extra + _dot(term, jnp.where(head_col == i, 1.0, 0.0).astype(BF16))
    _rms_heads_aug(acc, gain_row, 1.0,
                   lambda j: extra[:, j * HEAD_DIM:(j + 1) * HEAD_DIM], outs[0])


def _ep_cast(acc, extras, outs):
    outs[0][...] = acc.astype(outs[0].dtype)


def _ep_sigmoid(acc, extras, outs):
    outs[0][...] = jax.nn.sigmoid(acc).astype(outs[0].dtype)


def _ep_silu(acc, extras, outs):
    outs[0][...] = _silu(acc).astype(outs[0].dtype)


def _ep_hgrn_q(acc, extras, outs):
    outs[0][...] = (_silu(acc) * (HEAD_DIM ** -0.5)).astype(outs[0].dtype)


def _ep_hgrn_logf(acc, extras, outs):
    lb = extras[0]
    outs[0][...] = jnp.log(lb + (1.0 - lb) * jax.nn.sigmoid(acc))


_DSA_S_LAT = 0
_DSA_S_QI = DSA_LATENT
_DSA_S_KE = _DSA_S_QI + IDX_HEADS * IDX_DIM
_DSA_S_KO = _DSA_S_KE + 2 * IDX_DIM
_DSA_S_WI = _DSA_S_KO + 2 * IDX_DIM
_DSA_S_N = _DSA_S_WI + 128


def _ep_dsa_small(acc, extras, outs):
    lat_ref, qi_ref, ke_ref, ko_ref, wi_ref = outs
    _rms_heads(acc[:, :DSA_LATENT], extras[0][:, :DSA_LATENT], DSA_LATENT, 1.0, lat_ref)
    qi_ref[...] = acc[:, _DSA_S_QI:_DSA_S_KE].astype(BF16)
    ke_ref[...] = acc[:, _DSA_S_KE:_DSA_S_KO].astype(BF16)
    ko_ref[...] = acc[:, _DSA_S_KO:_DSA_S_WI].astype(BF16)
    wi_ref[...] = acc[:, _DSA_S_WI:_DSA_S_WI + IDX_HEADS] * (IDX_HEADS ** -0.5 * IDX_DIM ** -0.5)


def _out_kernel(a_ref, w_ref, x_ref, g_ref, o_ref):
    o_ref[...] = x_ref[...] + g_ref[0] * _dot(a_ref[...], w_ref[...])


def _out_proj(a, w, x2, gate, seq, tm, tn=1024):
    t, k = a.shape
    d = w.shape[1]
    bsz = gate.shape[0]
    per_seq = seq // tm
    return pl.pallas_call(
        _out_kernel,
        grid=(d // tn, t // tm),
        in_specs=[
            pl.BlockSpec((tm, k), lambda j, m: (m, 0)),
            pl.BlockSpec((k, tn), lambda j, m: (0, j)),
            pl.BlockSpec((tm, tn), lambda j, m: (m, j)),
            pl.BlockSpec((1, 1, tn), lambda j, m: (m // per_seq, 0, j)),
        ],
        out_specs=pl.BlockSpec((tm, tn), lambda j, m: (m, j)),
        out_shape=jax.ShapeDtypeStruct((t, d), F32),
        compiler_params=_params("parallel", "parallel"),
        name="out_proj",
    )(a, w, x2, gate.reshape(bsz, 1, d))


def _ffn_in_kernel(h_ref, wu_ref, wg_ref, cw_ref, cb_ref, a_ref, carry_ref, wub_ref, wgb_ref,
                   *, per_seq, rc, cc):
    m = pl.program_id(1)
    tm, tn = a_ref.shape

    @pl.when(m == 0)
    def _():
        wub_ref[...] = wu_ref[0].astype(BF16)
        wgb_ref[...] = wg_ref[0].astype(BF16)

    @pl.when(m % per_seq == 0)
    def _():
        carry_ref[...] = jnp.zeros_like(carry_ref)

    cw = cw_ref[0]
    cb = cb_ref[0]
    row = lax.broadcasted_iota(jnp.int32, (rc, cc), 0)
    for c in range(tn // cc):
        cols = slice(c * cc, (c + 1) * cc)
        prev = carry_ref[:, cols]
        for r in range(tm // rc):
            rows = slice(r * rc, (r + 1) * rc)
            h = h_ref[rows, :]
            u = _dot(h, wub_ref[:, cols])
            g = _dot(h, wgb_ref[:, cols])
            p1 = prev[7:8, :]
            p2 = prev[6:7, :]
            g1 = jnp.where(row == 0, p1, pltpu.roll(g, 1, axis=0))
            g2 = jnp.where(row == 0, p2, jnp.where(row == 1, p1, pltpu.roll(g, 2, axis=0)))
            conv = cw[0:1, cols] * g2 + cw[1:2, cols] * g1 + cw[2:3, cols] * g + cb[:, cols]
            a_ref[rows, cols] = (_silu(conv) * u).astype(BF16)
            prev = g[rc - 8:, :]
        carry_ref[:, cols] = prev


def _ffn_in(h, w_all, conv_w, conv_b, layer, seq, tm=2048, tn=512, rc=512, cc=256):
    t, k = h.shape
    f = w_all.shape[2] // 2
    nf = f // tn
    tm = min(tm, seq)
    per_seq = seq // tm
    rc = min(rc, tm)
    return pl.pallas_call(
        functools.partial(_ffn_in_kernel, per_seq=per_seq, rc=rc, cc=cc),
        grid=(nf, t // tm),
        in_specs=[
            pl.BlockSpec((tm, k), lambda j, m: (m, 0)),
            pl.BlockSpec((1, k, tn), lambda j, m: (layer, 0, j)),
            pl.BlockSpec((1, k, tn), lambda j, m: (layer, 0, j + nf)),
            pl.BlockSpec((1, CONV_WIDTH, tn), lambda j, m: (layer, 0, j)),
            pl.BlockSpec((1, 1, tn), lambda j, m: (layer, 0, j)),
        ],
        out_specs=pl.BlockSpec((tm, tn), lambda j, m: (m, j)),
        out_shape=jax.ShapeDtypeStruct((t, f), BF16),
        scratch_shapes=[
            pltpu.VMEM((8, tn), F32),
            pltpu.VMEM((k, tn), BF16),
            pltpu.VMEM((k, tn), BF16),
        ],
        compiler_params=_params("parallel", "arbitrary"),
        name="ffn_in",
    )(h, w_all, w_all, conv_w, conv_b)


def _dsa_select_kernel(qi_ref, ke_ref, ko_ref, wit_ref, bias_ref, key_ref, *, topk, ts):
    i = pl.program_id(1)
    tq = qi_ref.shape[0]
    seq = ke_ref.shape[0]
    n_tiles = ((i + 1) * tq + ts - 1) // ts
    qi = qi_ref[...]
    lhs = jnp.concatenate([qi[:, 128 * j:128 * (j + 1)] for j in range(IDX_HEADS // 2)], axis=0)
    wit = wit_ref[...]
    tpos = i * tq + lax.broadcasted_iota(jnp.int32, (ts, tq), 1)
    first_hidden = (tpos // CHUNK + 1) * CHUNK
    row = lax.broadcasted_iota(jnp.int32, (ts, tq), 0)

    def score_tile(t, carry):
        k0 = pl.multiple_of(t * ts, ts)
        sce = _dot_nt(ke_ref[pl.ds(k0, ts), :], lhs)
        sco = _dot_nt(ko_ref[pl.ds(k0, ts), :], lhs)
        score = jnp.zeros((ts, tq), F32)
        for j in range(IDX_HEADS // 2):
            cols = slice(j * tq, (j + 1) * tq)
            score = score + wit[2 * j:2 * j + 1] * jnp.maximum(sce[:, cols], 0.0)
            score = score + wit[2 * j + 1:2 * j + 2] * jnp.maximum(sco[:, cols], 0.0)
        score = score + 0.0
        bits = pltpu.bitcast(score, jnp.int32)
        key = jnp.where(bits < 0, bits ^ jnp.int32(0x7FFFFFFF), bits)
        key_ref[pl.ds(k0, ts), :] = jnp.where(k0 + row < first_hidden, key, INT_MIN)
        return carry

    lax.fori_loop(0, n_tiles, score_tile, 0)
    kf = jnp.float32(topk)

    def count_ge(cand):
        def body(t, cnt):
            k0 = pl.multiple_of(t * ts, ts)
            hit = jnp.where(key_ref[pl.ds(k0, ts), :] >= cand, 1.0, 0.0)
            return cnt + jnp.sum(hit, axis=0, keepdims=True)
        return lax.fori_loop(0, n_tiles, body, jnp.zeros((1, tq), F32))

    thr = jnp.where(count_ge(jnp.zeros((1, tq), jnp.int32)) >= kf, 0, INT_MIN).astype(jnp.int32)

    def bit_step(b, thr):
        cand = thr + lax.shift_left(jnp.int32(1), jnp.int32(30) - b)
        return jnp.where(count_ge(cand) >= kf, cand, thr)

    thr = lax.fori_loop(0, 31, bit_step, thr)

    def write_tile(t, carry):
        k0 = pl.multiple_of(t * ts, ts)
        key = key_ref[pl.ds(k0, ts), :]
        keep = jnp.where(key >= thr, jnp.where(key > INT_MIN, 0.0, NEG_BIG), NEG_BIG)
        bias_ref[0, pl.ds(k0, ts), :] = keep.astype(BF16)
        return carry

    def fill_tile(t, carry):
        k0 = pl.multiple_of(t * ts, ts)
        bias_ref[0, pl.ds(k0, ts), :] = jnp.full((ts, tq), NEG_BIG, BF16)
        return carry

    lax.fori_loop(0, n_tiles, write_tile, 0)
    lax.fori_loop(n_tiles, seq // ts, fill_tile, 0)


def _dsa_select(qi, ke, ko, wit, bsz, seq, tq=256, ts=256):
    nq = seq // tq
    topk = min(TOPK_MAX, seq // 4)
    return pl.pallas_call(
        functools.partial(_dsa_select_kernel, topk=topk, ts=ts),
        grid=(bsz, nq),
        in_specs=[
            pl.BlockSpec((tq, IDX_HEADS * IDX_DIM), lambda b, i: (b * nq + i, 0)),
            pl.BlockSpec((seq, 2 * IDX_DIM), lambda b, i: (b, 0)),
            pl.BlockSpec((seq, 2 * IDX_DIM), lambda b, i: (b, 0)),
            pl.BlockSpec((IDX_HEADS, tq), lambda b, i: (0, b * nq + i)),
        ],
        out_specs=pl.BlockSpec((1, seq, tq), lambda b, i: (b, 0, i)),
        out_shape=jax.ShapeDtypeStruct((bsz, seq, seq), BF16),
        scratch_shapes=[pltpu.VMEM((seq, tq), jnp.int32)],
        compiler_params=_params("parallel", "parallel"),
        name="dsa_select",
    )(qi, ke, ko, wit)


def _dsa_attn_kernel(q_ref, lat_ref, latt_ref, bias_ref, o_ref, m_ref, l_ref, acc_ref, *, hg, ng, tk):
    i = pl.program_id(1)
    tq = q_ref.shape[0]
    r = DSA_LATENT
    n_tiles = ((i + 1) * tq + tk - 1) // tk
    cols = hg * tq
    tpos = i * tq + lax.broadcasted_iota(jnp.int32, (tk, tq), 1)
    row = lax.broadcasted_iota(jnp.int32, (tk, tq), 0)
    for g0 in range(0, DSA_HEADS // hg, ng):
        groups = [range((g0 + n) * hg, (g0 + n + 1) * hg) for n in range(ng)]
        q4 = [jnp.concatenate([q_ref[:, h * r:(h + 1) * r] for h in heads], axis=0)
              for heads in groups]
        m_ref[...] = jnp.full(m_ref.shape, NEG_BIG, F32)
        l_ref[...] = jnp.zeros(l_ref.shape, F32)
        acc_ref[...] = jnp.zeros(acc_ref.shape, F32)

        def body(j, carry):
            k0 = pl.multiple_of(j * tk, tk)
            lat = lat_ref[pl.ds(k0, tk), :]
            latt = latt_ref[:, pl.ds(k0, tk)]
            dist = jnp.abs(tpos - (k0 + row)).astype(F32)
            bias = bias_ref[0, pl.ds(k0, tk), :].astype(F32)
            qk = [_dot_nt(lat, q4[n]) for n in range(ng)]
            ps, alphas = [], []
            for n, heads in enumerate(groups):
                base = jnp.concatenate(
                    [bias - (2.0 ** (-8.0 * (h + 1) / DSA_HEADS)) * dist for h in heads], axis=1)
                s = qk[n] + base
                m_old = m_ref[n]
                m_new = jnp.maximum(m_old, jnp.max(s, axis=0, keepdims=True))
                alpha = jnp.exp(m_old - m_new)
                p = jnp.exp(s - m_new)
                l_ref[n] = alpha * l_ref[n] + jnp.sum(p, axis=0, keepdims=True)
                m_ref[n] = m_new
                ps.append(p.astype(BF16))
                alphas.append(alpha)
            pvs = [_dot(latt, ps[n]) for n in range(ng)]
            for n in range(ng):
                acc_ref[n] = alphas[n] * acc_ref[n] + pvs[n]
            return carry

        lax.fori_loop(0, n_tiles, body, 0)
        for n, heads in enumerate(groups):
            out = (acc_ref[n] / l_ref[n]).T
            for idx, h in enumerate(heads):
                o_ref[:, h * r:(h + 1) * r] = out[idx * tq:(idx + 1) * tq].astype(BF16)


def _dsa_attn(q, lat, latt, bias, bsz, seq, tq=128, hg=4, ng=2, tk=256):
    nq = seq // tq
    width = DSA_HEADS * DSA_LATENT
    return pl.pallas_call(
        functools.partial(_dsa_attn_kernel, hg=hg, ng=ng, tk=tk),
        grid=(bsz, nq),
        in_specs=[
            pl.BlockSpec((tq, width), lambda b, i: (b * nq + i, 0)),
            pl.BlockSpec((seq, DSA_LATENT), lambda b, i: (b, 0)),
            pl.BlockSpec((DSA_LATENT, seq), lambda b, i: (0, b)),
            pl.BlockSpec((1, seq, tq), lambda b, i: (b, 0, i)),
        ],
        out_specs=pl.BlockSpec((tq, width), lambda b, i: (b * nq + i, 0)),
        out_shape=jax.ShapeDtypeStruct((bsz * seq, width), BF16),
        scratch_shapes=[
            pltpu.VMEM((ng, 1, hg * tq), F32),
            pltpu.VMEM((ng, 1, hg * tq), F32),
            pltpu.VMEM((ng, DSA_LATENT, hg * tq), F32),
        ],
        compiler_params=_params("parallel", "parallel"),
        name="dsa_attn",
    )(q, lat, latt, bias)


def _split3(x):
    hi = x.astype(BF16)
    r1 = x - hi.astype(F32)
    mid = r1.astype(BF16)
    lo = (r1 - mid.astype(F32)).astype(BF16)
    return hi, mid, lo


def _fox_cum_kernel(h_ref, wt_ref, b_ref, cum_ref, *, blk):
    fl = _dot_nt(wt_ref[...], h_ref[...]) + b_ref[...]
    logf = jax.nn.log_sigmoid(fl)
    seq = logf.shape[1]
    r = lax.broadcasted_iota(jnp.int32, (blk, blk), 0)
    c = lax.broadcasted_iota(jnp.int32, (blk, blk), 1)
    upper = jnp.where(r <= c, 1.0, 0.0).astype(BF16)
    carry = jnp.zeros((logf.shape[0], 1), F32)
    for j in range(seq // blk):
        hi, mid, lo = _split3(logf[:, j * blk:(j + 1) * blk])
        cs = (_dot(hi, upper) + _dot(mid, upper)) + _dot(lo, upper) + carry
        cum_ref[0, :, j * blk:(j + 1) * blk] = cs
        carry = cs[:, blk - 1:blk]


def _fox_cum(h, w_fl_t, b_f, bsz, seq, blk=256):
    d = h.shape[1]
    return pl.pallas_call(
        functools.partial(_fox_cum_kernel, blk=blk),
        grid=(bsz,),
        in_specs=[
            pl.BlockSpec((seq, d), lambda b: (b, 0)),
            pl.BlockSpec((FOX_HEADS, d), lambda b: (0, 0)),
            pl.BlockSpec((FOX_HEADS, 1), lambda b: (0, 0)),
        ],
        out_specs=pl.BlockSpec((1, FOX_HEADS, seq), lambda b: (b, 0, 0)),
        out_shape=jax.ShapeDtypeStruct((bsz, FOX_HEADS, seq), F32),
        compiler_params=_params("parallel"),
        name="fox_cum",
    )(h, w_fl_t, b_f.reshape(FOX_HEADS, 1))


def _fox_attn_kernel(q_ref, k_ref, vt_ref, g_ref, o_ref, m_ref, l_ref, acc_ref):
    i = pl.program_id(2)
    tq = q_ref.shape[0]
    hb = g_ref.shape[1] // HEAD_DIM
    qs = [q_ref[:, h * FOX_AUG:(h + 1) * FOX_AUG] for h in range(hb)]
    m_ref[...] = jnp.full(m_ref.shape, NEG_BIG, F32)
    l_ref[...] = jnp.zeros(l_ref.shape, F32)
    acc_ref[...] = jnp.zeros(acc_ref.shape, F32)

    def tile(j, diagonal):
        k0 = pl.multiple_of(j * tq, tq)
        ss = [_dot_nt(k_ref[pl.ds(k0, tq), h * FOX_AUG:(h + 1) * FOX_AUG], qs[h])
              for h in range(hb)]
        ps, alphas = [], []
        for h in range(hb):
            s = ss[h]
            if diagonal:
                key = lax.broadcasted_iota(jnp.int32, (tq, tq), 0)
                qry = lax.broadcasted_iota(jnp.int32, (tq, tq), 1)
                s = jnp.where(key <= qry, s, NEG_BIG)
            m_old = m_ref[h]
            m_new = jnp.maximum(m_old, jnp.max(s, axis=0, keepdims=True))
            alpha = jnp.exp(m_old - m_new)
            p = jnp.exp(s - m_new)
            l_ref[h] = alpha * l_ref[h] + jnp.sum(p, axis=0, keepdims=True)
            m_ref[h] = m_new
            ps.append(p.astype(BF16))
            alphas.append(alpha)
        pvs = [_dot(vt_ref[h * HEAD_DIM:(h + 1) * HEAD_DIM, pl.ds(k0, tq)], ps[h])
               for h in range(hb)]
        for h in range(hb):
            acc_ref[h] = alphas[h] * acc_ref[h] + pvs[h]

    def body(j, carry):
        tile(j, False)
        return carry

    lax.fori_loop(0, i, body, 0)
    tile(i, True)
    for h in range(hb):
        cols = slice(h * HEAD_DIM, (h + 1) * HEAD_DIM)
        o = (acc_ref[h] / l_ref[h]).T
        o_ref[:, cols] = (o * g_ref[:, cols].astype(F32)).astype(BF16)


def _fox_attn(q, k, vt, gate, bsz, seq, tq=256, hb=4):
    nq = seq // tq
    d = gate.shape[1]
    qmap = lambda b, h, i: (b * nq + i, h)
    return pl.pallas_call(
        _fox_attn_kernel,
        grid=(bsz, FOX_HEADS // hb, nq),
        in_specs=[
            pl.BlockSpec((tq, hb * FOX_AUG), qmap),
            pl.BlockSpec((seq, hb * FOX_AUG), lambda b, h, i: (b, h)),
            pl.BlockSpec((hb * HEAD_DIM, seq), lambda b, h, i: (h, b)),
            pl.BlockSpec((tq, hb * HEAD_DIM), qmap),
        ],
        out_specs=pl.BlockSpec((tq, hb * HEAD_DIM), qmap),
        out_shape=jax.ShapeDtypeStruct((bsz * seq, d), BF16),
        scratch_shapes=[
            pltpu.VMEM((hb, 1, tq), F32),
            pltpu.VMEM((hb, 1, tq), F32),
            pltpu.VMEM((hb, HEAD_DIM, tq), F32),
        ],
        compiler_params=_params("parallel", "parallel", "parallel"),
        name="fox_attn",
    )(q, k, vt, gate)


HGRN_SUB = 8


def _hgrn_kernel(q_ref, lg_ref, v_ref, gate_ref, gain_ref, o_ref,
                 state_ref, oacc_ref, kpad_ref, fpad_ref, vpad_ref, *, hb):
    c = pl.program_id(2)
    cl = q_ref.shape[0]
    dk = HEAD_DIM
    sub = HGRN_SUB

    @pl.when(c == 0)
    def _():
        state_ref[...] = jnp.zeros_like(state_ref)
        kpad_ref[0:sub, :] = jnp.zeros((sub, dk), F32)
        fpad_ref[0:sub, :] = jnp.zeros((sub, dk), F32)
        vpad_ref[0:sub, :] = jnp.zeros((sub, dk), F32)

    r = lax.broadcasted_iota(jnp.int32, (cl, cl), 0)
    cc = lax.broadcasted_iota(jnp.int32, (cl, cl), 1)
    lower = jnp.where(cc <= r, 1.0, 0.0).astype(BF16)
    block_start = lax.broadcasted_iota(jnp.int32, (cl, dk), 0) % sub == 0

    for hh in range(hb):
        cols = slice(hh * dk, (hh + 1) * dk)
        q = q_ref[:, cols].astype(F32)
        lg = lg_ref[:, cols]
        vb = v_ref[:, cols]
        v = vb.astype(F32)
        hi, mid, lo = _split3(lg)
        gcum = (_dot(lower, hi) + _dot(lower, mid)) + _dot(lower, lo)
        f = jnp.exp(lg)
        k = 1.0 - f
        st = state_ref[hh]

        oacc_ref[...] = _dot_nt((q * jnp.exp(gcum)).astype(BF16), st.astype(BF16))

        m = cl // 2
        while m >= sub:
            for blk in range(cl // (2 * m)):
                r0 = blk * 2 * m
                mid_row = r0 + m
                gm = gcum[mid_row - 1:mid_row, :]
                qs = (q[mid_row:mid_row + m] * jnp.exp(gcum[mid_row:mid_row + m] - gm)).astype(BF16)
                ks = (k[r0:mid_row] * jnp.exp(gm - gcum[r0:mid_row])).astype(BF16)
                sc = _dot_nt(qs, ks)
                oacc_ref[mid_row:mid_row + m, :] += _dot(sc.astype(BF16), vb[r0:mid_row])
            m //= 2

        kpad_ref[sub:, :] = k
        fpad_ref[sub:, :] = jnp.where(block_start, 0.0, f)
        vpad_ref[sub:, :] = v
        diag = jnp.sum(q * k, axis=-1, keepdims=True) * v
        dec = None
        for delta in range(1, sub):
            fd = fpad_ref[sub - delta + 1:sub - delta + 1 + cl, :]
            dec = fd if dec is None else dec * fd
            kd = kpad_ref[sub - delta:sub - delta + cl, :]
            vd = vpad_ref[sub - delta:sub - delta + cl, :]
            diag = diag + jnp.sum(q * kd * dec, axis=-1, keepdims=True) * vd
        o = oacc_ref[...] + diag

        gl = gcum[cl - 1:cl, :]
        kdec = (k * jnp.exp(gl - gcum)).astype(BF16)
        state_ref[hh] = st * jnp.exp(gl) + _dot_tn(vb, kdec)

        ms = jnp.mean(o * o, axis=-1, keepdims=True)
        y = o * lax.rsqrt(ms + EPS) * gain_ref[...]
        o_ref[:, cols] = (y * gate_ref[:, cols].astype(F32)).astype(BF16)


def _hgrn_core(q, lg, v, gate, gain, bsz, seq, cl=256, hb=4):
    nc = seq // cl
    d = q.shape[1]
    blk = lambda b, g, c: (b * nc + c, g)
    spec = pl.BlockSpec((cl, hb * HEAD_DIM), blk)
    return pl.pallas_call(
        functools.partial(_hgrn_kernel, hb=hb),
        grid=(bsz, HGRN_HEADS // hb, nc),
        in_specs=[spec, spec, spec, spec, pl.BlockSpec((1, HEAD_DIM), lambda b, g, c: (0, 0))],
        out_specs=spec,
        out_shape=jax.ShapeDtypeStruct((bsz * seq, d), BF16),
        scratch_shapes=[
            pltpu.VMEM((hb, HEAD_DIM, HEAD_DIM), F32),
            pltpu.VMEM((cl, HEAD_DIM), F32),
            pltpu.VMEM((cl + HGRN_SUB, HEAD_DIM), F32),
            pltpu.VMEM((cl + HGRN_SUB, HEAD_DIM), F32),
            pltpu.VMEM((cl + HGRN_SUB, HEAD_DIM), F32),
        ],
        compiler_params=_params("parallel", "parallel", "arbitrary"),
        name="hgrn_core",
    )(q, lg, v, gate, gain.reshape(1, HEAD_DIM))


def _dsa_mixer(h, w_in, q_gain, kv_gain, bsz, seq):
    d = h.shape[1]
    nq = DSA_HEADS * DSA_LATENT
    lat_w = w_in[:, nq:nq + DSA_LATENT]
    qi_w = w_in[:, nq + DSA_LATENT:nq + DSA_LATENT + IDX_HEADS * IDX_DIM]
    c0 = nq + DSA_LATENT + IDX_HEADS * IDX_DIM
    ki_w = w_in[:, c0:c0 + IDX_DIM]
    wi_w = w_in[:, c0 + IDX_DIM:c0 + IDX_DIM + IDX_HEADS]
    z = jnp.zeros((d, IDX_DIM), F32)
    w_s = jnp.concatenate(
        [lat_w, qi_w, ki_w, z, z, ki_w, wi_w, jnp.zeros((d, 128 - IDX_HEADS), F32)], axis=1).astype(BF16)
    q_gain_row = jnp.tile(q_gain, DSA_HEADS).reshape(1, nq)
    kv_gain_row = jnp.zeros((1, _DSA_S_N), F32).at[0, :DSA_LATENT].set(kv_gain)

    (q,) = _proj(h, w_in, 0, nq, _ep_dsa_q, [(1024, nq, BF16)], extras=(q_gain_row,))
    lat, qi, ke, ko, wi = _proj(
        h, w_s, 0, _DSA_S_N, _ep_dsa_small,
        [(DSA_LATENT, DSA_LATENT, BF16), (IDX_HEADS * IDX_DIM, IDX_HEADS * IDX_DIM, BF16),
         (2 * IDX_DIM, 2 * IDX_DIM, BF16), (2 * IDX_DIM, 2 * IDX_DIM, BF16), (IDX_HEADS, IDX_HEADS, F32)],
        extras=(kv_gain_row,), tn=_DSA_S_N)
    bias = _dsa_select(qi, ke, ko, wi.T, bsz, seq)
    return _dsa_attn(q, lat, lat.T, bias, bsz, seq)


def _fox_mixer(h, w_in, b_f, q_gain, k_gain, bsz, seq):
    d = h.shape[1]
    w = w_in
    w_fl_t = w_in[:, 4 * d:].T.astype(BF16)
    qg = jnp.tile(q_gain, FOX_HEADS).reshape(1, d)
    kg = jnp.tile(k_gain, FOX_HEADS).reshape(1, d)
    cum = _fox_cum(h, w_fl_t, b_f, bsz, seq)
    cum_t = cum.transpose(0, 2, 1).reshape(bsz * seq, FOX_HEADS)
    aug = [(1024 // HEAD_DIM * FOX_AUG, FOX_HEADS * FOX_AUG, BF16)]
    (q,) = _proj(h, w, 0, d, _ep_fox_q, aug, extras=(qg,))
    (k,) = _proj(h, w, d, d, _ep_fox_k, aug, extras=(kg,), row_extras=(cum_t,))
    vt = _proj_t(h, w_in[:, 2 * d:3 * d].T.astype(BF16))
    (gate,) = _proj(h, w, 3 * d, d, _ep_sigmoid, [(1024, d, BF16)])
    return _fox_attn(q, k, vt, gate, bsz, seq)


def _hgrn_mixer(h, w_in, lb, o_gain, bsz, seq):
    d = h.shape[1]
    w = w_in
    (q,) = _proj(h, w, 0, d, _ep_hgrn_q, [(1024, d, BF16)])
    (lg,) = _proj(h, w, d, d, _ep_hgrn_logf, [(1024, d, F32)], extras=(lb.reshape(1, d),))
    (v,) = _proj(h, w, 2 * d, d, _ep_cast, [(1024, d, BF16)])
    (gate,) = _proj(h, w, 3 * d, d, _ep_silu, [(1024, d, BF16)])
    return _hgrn_core(q, lg, v, gate, o_gain, bsz, seq)


def kernel(x, c, ada_w, ada_b, norm_mix_g, norm_ffn_g, dsa_w_in, dsa_q_norm, dsa_kv_norm, dsa_w_out, fox_w_in, fox_b_f, fox_q_norm, fox_k_norm, fox_w_out, hgrn_w_in, hgrn_lb, hgrn_o_norm, hgrn_w_out, ffn_w_in, ffn_conv_w, ffn_conv_b, ffn_w_out):
    bsz, seq, d = x.shape
    depth = ada_w.shape[0]
    f = ffn_conv_b.shape[1]

    mod = _ada_all(c, ada_w, ada_b)
    lb_soft = jax.nn.softmax(hgrn_lb.astype(F32), axis=0)
    lb_all = jnp.cumsum(lb_soft, axis=0) - lb_soft[0]
    conv_b = ffn_conv_b.reshape(depth, 1, f)

    x2 = x.reshape(bsz * seq, d)
    for i in range(depth):
        sh1, sc1, g1, sh2, sc2, g2 = [mod[i, :, j * d:(j + 1) * d] for j in range(6)]
        h = _norm_mod(x2, norm_mix_g[i], sh1, sc1, seq)
        kind, j = i % 3, i // 3
        if kind == 0:
            a = _dsa_mixer(h, dsa_w_in[j], dsa_q_norm[j], dsa_kv_norm[j], bsz, seq)
            w_out = dsa_w_out[j]
        elif kind == 1:
            a = _fox_mixer(h, fox_w_in[j], fox_b_f[j], fox_q_norm[j], fox_k_norm[j], bsz, seq)
            w_out = fox_w_out[j]
        else:
            a = _hgrn_mixer(h, hgrn_w_in[j], lb_all[i], hgrn_o_norm[j], bsz, seq)
            w_out = hgrn_w_out[j]
        x2 = _out_proj(a, w_out.astype(BF16), x2, g1, seq, tm=1024)
        h = _norm_mod(x2, norm_ffn_g[i], sh2, sc2, seq)
        a = _ffn_in(h, ffn_w_in, ffn_conv_w, conv_b, i, seq)
        x2 = _out_proj(a, ffn_w_out[i].astype(BF16), x2, g2, seq, tm=512)
    return x2.reshape(bsz, seq, d)
```

```python
import functools

import jax
import jax.numpy as jnp
from jax import lax
from jax.experimental import pallas as pl
from jax.experimental.pallas import tpu as pltpu

F32 = jnp.float32
BF16 = jnp.bfloat16

EPS = 1e-6
CHUNK = 64
TOPK_MAX = 256

DSA_HEADS = 16
DSA_LATENT = 256
IDX_HEADS = 16
IDX_DIM = 64
FOX_HEADS = 16
HGRN_HEADS = 16
HEAD_DIM = 128
CONV_WIDTH = 3

NEG_BIG = -1e30
LOG2E = 1.4426950408889634
ONES_ROWS = 16
INT_MIN = -2 ** 31

V7X_VMEM_BYTES = 64 * 1024 * 1024
VMEM_LIMIT = 56 * 1024 * 1024


def _params(*sem):
    return pltpu.CompilerParams(dimension_semantics=sem, vmem_limit_bytes=VMEM_LIMIT)


def _dot(a, b):
    return jnp.dot(a, b, preferred_element_type=F32)


def _dot_nt(a, b):
    return lax.dot_general(a, b, (((1,), (1,)), ((), ())), preferred_element_type=F32)


def _dot_tn(a, b):
    return lax.dot_general(a, b, (((0,), (0,)), ((), ())), preferred_element_type=F32)


def _silu(x):
    return x * jax.nn.sigmoid(x)


def _ada_kernel(c_ref, w_ref, b_ref, o_ref):
    cond = _silu(c_ref[...]).astype(BF16)
    o_ref[0] = _dot(cond, w_ref[0].astype(BF16)) + b_ref[0]


def _ada_all(c, ada_w, ada_b, tn=1024):
    depth, d, n = ada_w.shape
    bsz = c.shape[0]
    rows = 8
    c_pad = jnp.zeros((rows, d), F32).at[:bsz].set(c)
    out = pl.pallas_call(
        _ada_kernel,
        grid=(depth, n // tn),
        in_specs=[
            pl.BlockSpec((rows, d), lambda l, j: (0, 0)),
            pl.BlockSpec((1, d, tn), lambda l, j: (l, 0, j)),
            pl.BlockSpec((1, 1, tn), lambda l, j: (l, 0, j)),
        ],
        out_specs=pl.BlockSpec((1, rows, tn), lambda l, j: (l, 0, j)),
        out_shape=jax.ShapeDtypeStruct((depth, rows, n), F32),
        compiler_params=_params("parallel", "parallel"),
        name="ada_mod",
    )(c_pad, ada_w, ada_b.reshape(depth, 1, n))
    return out[:, :bsz]


def _norm_mod_kernel(x_ref, g_ref, sh_ref, sc_ref, h_ref):
    x = x_ref[...]
    ms = jnp.mean(x * x, axis=-1, keepdims=True)
    y = x * lax.rsqrt(ms + EPS) * g_ref[...]
    h_ref[...] = (y * (1.0 + sc_ref[0]) + sh_ref[0]).astype(BF16)


def _norm_mod(x2, g, shift, scale, seq, tm=512):
    t, d = x2.shape
    per_seq = seq // tm
    bsz = shift.shape[0]
    vec = lambda m: (m // per_seq, 0, 0)
    return pl.pallas_call(
        _norm_mod_kernel,
        grid=(t // tm,),
        in_specs=[
            pl.BlockSpec((tm, d), lambda m: (m, 0)),
            pl.BlockSpec((1, d), lambda m: (0, 0)),
            pl.BlockSpec((1, 1, d), vec),
            pl.BlockSpec((1, 1, d), vec),
        ],
        out_specs=pl.BlockSpec((tm, d), lambda m: (m, 0)),
        out_shape=jax.ShapeDtypeStruct((t, d), BF16),
        compiler_params=_params("parallel"),
        name="norm_mod",
    )(x2, g.reshape(1, d), shift.reshape(bsz, 1, d), scale.reshape(bsz, 1, d))


def _proj_kernel(*refs, epilogue, n_extra, cast_w):
    h_ref, w_ref = refs[0], refs[1]
    extras = refs[2:2 + n_extra]
    if cast_w:
        outs, wb_ref = refs[2 + n_extra:-1], refs[-1]

        @pl.when(pl.program_id(1) == 0)
        def _():
            wb_ref[...] = w_ref[...].astype(BF16)

        w = wb_ref[...]
    else:
        outs = refs[2 + n_extra:]
        w = w_ref[...]
    acc = _dot(h_ref[...], w)
    epilogue(acc, [e[...] for e in extras], outs)


def _proj(h, w, col0, n, epilogue, out_defs, extras=(), row_extras=(), tm=1024, tn=1024):
    t, k = h.shape
    tn = min(tn, n)
    tm = min(tm, t)
    assert n % tn == 0 and col0 % tn == 0 and t % tm == 0
    off = col0 // tn
    in_specs = [
        pl.BlockSpec((tm, k), lambda j, m: (m, 0)),
        pl.BlockSpec((k, tn), lambda j, m: (0, j + off)),
    ] + [pl.BlockSpec((1, tn), lambda j, m: (0, j)) for _ in extras
         ] + [pl.BlockSpec((tm, r.shape[1]), lambda j, m: (m, 0)) for r in row_extras]
    extras = tuple(extras) + tuple(row_extras)
    out_specs = [pl.BlockSpec((tm, wt), lambda j, m: (m, j)) for wt, _, _ in out_defs]
    out_shape = [jax.ShapeDtypeStruct((t, wtot), dt) for _, wtot, dt in out_defs]
    cast_w = w.dtype != BF16
    return pl.pallas_call(
        functools.partial(_proj_kernel, epilogue=epilogue, n_extra=len(extras), cast_w=cast_w),
        grid=(n // tn, t // tm),
        in_specs=in_specs,
        out_specs=out_specs,
        out_shape=out_shape,
        scratch_shapes=[pltpu.VMEM((k, tn), BF16)] if cast_w else [],
        compiler_params=_params("parallel", "arbitrary"),
        name="proj_" + epilogue.__name__.strip("_"),
    )(h, w, *extras)


def _proj_t_kernel(wt_ref, h_ref, o_ref):
    o_ref[...] = _dot_nt(wt_ref[...], h_ref[...]).astype(o_ref.dtype)


def _proj_t(h, wt, tm=1024, tn=1024):
    t, k = h.shape
    n = wt.shape[0]
    tm = min(tm, t)
    return pl.pallas_call(
        _proj_t_kernel,
        grid=(n // tn, t // tm),
        in_specs=[
            pl.BlockSpec((tn, k), lambda j, m: (j, 0)),
            pl.BlockSpec((tm, k), lambda j, m: (m, 0)),
        ],
        out_specs=pl.BlockSpec((tn, tm), lambda j, m: (j, m)),
        out_shape=jax.ShapeDtypeStruct((n, t), BF16),
        compiler_params=_params("parallel", "parallel"),
        name="proj_t",
    )(wt, h)


def _rms_heads(acc, gain_row, hd, scale, o_ref):
    for j in range(acc.shape[1] // hd):
        seg = acc[:, j * hd:(j + 1) * hd]
        ms = jnp.mean(seg * seg, axis=-1, keepdims=True)
        g = gain_row[:, j * hd:(j + 1) * hd]
        if scale != 1.0:
            g = g * scale
        o_ref[:, j * hd:(j + 1) * hd] = (seg * lax.rsqrt(ms + EPS) * g).astype(o_ref.dtype)


def _ep_dsa_q(acc, extras, outs):
    _rms_heads(acc, extras[0], DSA_LATENT, DSA_LATENT ** -0.5 * LOG2E, outs[0])


FOX_AUG = 2 * HEAD_DIM
FOX_BIAS_LANES = 3


def _rms_heads_aug(acc, gain_row, scale, extra_fn, o_ref):
    for j in range(acc.shape[1] // HEAD_DIM):
        cols = slice(j * HEAD_DIM, (j + 1) * HEAD_DIM)
        seg = acc[:, cols]
        ms = jnp.mean(seg * seg, axis=-1, keepdims=True)
        y = seg * lax.rsqrt(ms + EPS) * (gain_row[:, cols] * scale)
        o_ref[:, j * FOX_AUG:j * FOX_AUG + HEAD_DIM] = y.astype(o_ref.dtype)
        o_ref[:, j * FOX_AUG + HEAD_DIM:(j + 1) * FOX_AUG] = extra_fn(j).astype(o_ref.dtype)


def _ep_fox_q(acc, extras, outs):
    lane = lax.broadcasted_iota(jnp.int32, (acc.shape[0], HEAD_DIM), 1)
    ones = jnp.where(lane < FOX_BIAS_LANES, 1.0, 0.0)
    _rms_heads_aug(acc, extras[0], HEAD_DIM ** -0.5 * LOG2E, lambda j: ones, outs[0])


def _ep_fox_k(acc, extras, outs):
    gain_row, cum_t = extras
    heads = acc.shape[1] // HEAD_DIM
    first = pl.program_id(0) * heads
    r = lax.broadcasted_iota(jnp.int32, (FOX_HEADS, acc.shape[1]), 0)
    c = lax.broadcasted_iota(jnp.int32, (FOX_HEADS, acc.shape[1]), 1)
    head_col = jnp.where(r == first + c // HEAD_DIM, c % HEAD_DIM, -1)
    extra = jnp.zeros(acc.shape, F32)
    for i, term in enumerate(_split3(cum_t * (-LOG2E))):
        extra = extra + _dot(term, jnp.where(head_col == i, 1.0, 0.0).astype(BF16))
    _rms_heads_aug(acc, gain_row, 1.0,
                   lambda j: extra[:, j * HEAD_DIM:(j + 1) * HEAD_DIM], outs[0])


def _ep_cast(acc, extras, outs):
    outs[0][...] = acc.astype(outs[0].dtype)


def _ep_sigmoid(acc, extras, outs):
    outs[0][...] = jax.nn.sigmoid(acc).astype(outs[0].dtype)


def _ep_silu(acc, extras, outs):
    outs[0][...] = _silu(acc).astype(outs[0].dtype)


def _ep_hgrn_q(acc, extras, outs):
    outs[0][...] = (_silu(acc) * (HEAD_DIM ** -0.5)).astype(outs[0].dtype)


def _ep_hgrn_logf(acc, extras, outs):
    lb = extras[0]
    outs[0][...] = jnp.log(lb + (1.0 - lb) * jax.nn.sigmoid(acc))


_DSA_S_LAT = 0
_DSA_S_QI = DSA_LATENT
_DSA_S_KE = _DSA_S_QI + IDX_HEADS * IDX_DIM
_DSA_S_KO = _DSA_S_KE + 2 * IDX_DIM
_DSA_S_WI = _DSA_S_KO + 2 * IDX_DIM
_DSA_S_N = _DSA_S_WI + 128


def _ep_dsa_small(acc, extras, outs):
    lat_ref, qi_ref, ke_ref, ko_ref, wi_ref = outs
    _rms_heads(acc[:, :DSA_LATENT], extras[0][:, :DSA_LATENT], DSA_LATENT, 1.0, lat_ref)
    qi_ref[...] = acc[:, _DSA_S_QI:_DSA_S_KE].astype(BF16)
    ke_ref[...] = acc[:, _DSA_S_KE:_DSA_S_KO].astype(BF16)
    ko_ref[...] = acc[:, _DSA_S_KO:_DSA_S_WI].astype(BF16)
    wi_ref[...] = acc[:, _DSA_S_WI:_DSA_S_WI + IDX_HEADS] * (IDX_HEADS ** -0.5 * IDX_DIM ** -0.5)


def _out_kernel(a_ref, w_ref, x_ref, g_ref, o_ref):
    o_ref[...] = x_ref[...] + g_ref[0] * _dot(a_ref[...], w_ref[...])


def _out_proj(a, w, x2, gate, seq, tm, tn=1024):
    t, k = a.shape
    d = w.shape[1]
    bsz = gate.shape[0]
    per_seq = seq // tm
    return pl.pallas_call(
        _out_kernel,
        grid=(d // tn, t // tm),
        in_specs=[
            pl.BlockSpec((tm, k), lambda j, m: (m, 0)),
            pl.BlockSpec((k, tn), lambda j, m: (0, j)),
            pl.BlockSpec((tm, tn), lambda j, m: (m, j)),
            pl.BlockSpec((1, 1, tn), lambda j, m: (m // per_seq, 0, j)),
        ],
        out_specs=pl.BlockSpec((tm, tn), lambda j, m: (m, j)),
        out_shape=jax.ShapeDtypeStruct((t, d), F32),
        compiler_params=_params("parallel", "parallel"),
        name="out_proj",
    )(a, w, x2, gate.reshape(bsz, 1, d))


def _ffn_in_kernel(h_ref, wu_ref, wg_ref, cw_ref, cb_ref, a_ref, carry_ref, wub_ref, wgb_ref,
                   *, per_seq, rc, cc):
    m = pl.program_id(1)
    tm, tn = a_ref.shape

    @pl.when(m == 0)
    def _():
        wub_ref[...] = wu_ref[0].astype(BF16)
        wgb_ref[...] = wg_ref[0].astype(BF16)

    @pl.when(m % per_seq == 0)
    def _():
        carry_ref[...] = jnp.zeros_like(carry_ref)

    cw = cw_ref[0]
    cb = cb_ref[0]
    row = lax.broadcasted_iota(jnp.int32, (rc, cc), 0)
    for c in range(tn // cc):
        cols = slice(c * cc, (c + 1) * cc)
        prev = carry_ref[:, cols]
        for r in range(tm // rc):
            rows = slice(r * rc, (r + 1) * rc)
            h = h_ref[rows, :]
            u = _dot(h, wub_ref[:, cols])
            g = _dot(h, wgb_ref[:, cols])
            p1 = prev[7:8, :]
            p2 = prev[6:7, :]
            g1 = jnp.where(row == 0, p1, pltpu.roll(g, 1, axis=0))
            g2 = jnp.where(row == 0, p2, jnp.where(row == 1, p1, pltpu.roll(g, 2, axis=0)))
            conv = cw[0:1, cols] * g2 + cw[1:2, cols] * g1 + cw[2:3, cols] * g + cb[:, cols]
            a_ref[rows, cols] = (_silu(conv) * u).astype(BF16)
            prev = g[rc - 8:, :]
        carry_ref[:, cols] = prev


def _ffn_in(h, w_all, conv_w, conv_b, layer, seq, tm=2048, tn=512, rc=512, cc=256):
    t, k = h.shape
    f = w_all.shape[2] // 2
    nf = f // tn
    tm = min(tm, seq)
    per_seq = seq // tm
    rc = min(rc, tm)
    return pl.pallas_call(
        functools.partial(_ffn_in_kernel, per_seq=per_seq, rc=rc, cc=cc),
        grid=(nf, t // tm),
        in_specs=[
            pl.BlockSpec((tm, k), lambda j, m: (m, 0)),
            pl.BlockSpec((1, k, tn), lambda j, m: (layer, 0, j)),
            pl.BlockSpec((1, k, tn), lambda j, m: (layer, 0, j + nf)),
            pl.BlockSpec((1, CONV_WIDTH, tn), lambda j, m: (layer, 0, j)),
            pl.BlockSpec((1, 1, tn), lambda j, m: (layer, 0, j)),
        ],
        out_specs=pl.BlockSpec((tm, tn), lambda j, m: (m, j)),
        out_shape=jax.ShapeDtypeStruct((t, f), BF16),
        scratch_shapes=[
            pltpu.VMEM((8, tn), F32),
            pltpu.VMEM((k, tn), BF16),
            pltpu.VMEM((k, tn), BF16),
        ],
        compiler_params=_params("parallel", "arbitrary"),
        name="ffn_in",
    )(h, w_all, w_all, conv_w, conv_b)


def _dsa_select_kernel(qi_ref, ke_ref, ko_ref, wit_ref, bias_ref, key_ref, *, topk, ts):
    i = pl.program_id(1)
    tq = qi_ref.shape[0]
    seq = ke_ref.shape[0]
    n_tiles = ((i + 1) * tq + ts - 1) // ts
    qi = qi_ref[...]
    lhs = jnp.concatenate([qi[:, 128 * j:128 * (j + 1)] for j in range(IDX_HEADS // 2)], axis=0)
    wit = wit_ref[...]
    tpos = i * tq + lax.broadcasted_iota(jnp.int32, (ts, tq), 1)
    first_hidden = (tpos // CHUNK + 1) * CHUNK
    row = lax.broadcasted_iota(jnp.int32, (ts, tq), 0)

    def score_tile(t, carry):
        k0 = pl.multiple_of(t * ts, ts)
        sce = _dot_nt(ke_ref[pl.ds(k0, ts), :], lhs)
        sco = _dot_nt(ko_ref[pl.ds(k0, ts), :], lhs)
        score = jnp.zeros((ts, tq), F32)
        for j in range(IDX_HEADS // 2):
            cols = slice(j * tq, (j + 1) * tq)
            score = score + wit[2 * j:2 * j + 1] * jnp.maximum(sce[:, cols], 0.0)
            score = score + wit[2 * j + 1:2 * j + 2] * jnp.maximum(sco[:, cols], 0.0)
        score = score + 0.0
        bits = pltpu.bitcast(score, jnp.int32)
        key = jnp.where(bits < 0, bits ^ jnp.int32(0x7FFFFFFF), bits)
        key_ref[pl.ds(k0, ts), :] = jnp.where(k0 + row < first_hidden, key, INT_MIN)
        return carry

    lax.fori_loop(0, n_tiles, score_tile, 0)
    kf = jnp.float32(topk)

    def count_ge(cand):
        def body(t, cnt):
            k0 = pl.multiple_of(t * ts, ts)
            hit = jnp.where(key_ref[pl.ds(k0, ts), :] >= cand, 1.0, 0.0)
            return cnt + jnp.sum(hit, axis=0, keepdims=True)
        return lax.fori_loop(0, n_tiles, body, jnp.zeros((1, tq), F32))

    thr = jnp.where(count_ge(jnp.zeros((1, tq), jnp.int32)) >= kf, 0, INT_MIN).astype(jnp.int32)

    def bit_step(b, thr):
        cand = thr + lax.shift_left(jnp.int32(1), jnp.int32(30) - b)
        return jnp.where(count_ge(cand) >= kf, cand, thr)

    thr = lax.fori_loop(0, 31, bit_step, thr)

    def write_tile(t, carry):
        k0 = pl.multiple_of(t * ts, ts)
        key = key_ref[pl.ds(k0, ts), :]
        keep = jnp.where(key >= thr, jnp.where(key > INT_MIN, 0.0, NEG_BIG), NEG_BIG)
        bias_ref[0, pl.ds(k0, ts), :] = keep.astype(BF16)
        return carry

    def fill_tile(t, carry):
        k0 = pl.multiple_of(t * ts, ts)
        bias_ref[0, pl.ds(k0, ts), :] = jnp.full((ts, tq), NEG_BIG, BF16)
        return carry

    lax.fori_loop(0, n_tiles, write_tile, 0)
    lax.fori_loop(n_tiles, seq // ts, fill_tile, 0)


def _dsa_select(qi, ke, ko, wit, bsz, seq, tq=256, ts=256):
    nq = seq // tq
    topk = min(TOPK_MAX, seq // 4)
    return pl.pallas_call(
        functools.partial(_dsa_select_kernel, topk=topk, ts=ts),
        grid=(bsz, nq),
        in_specs=[
            pl.BlockSpec((tq, IDX_HEADS * IDX_DIM), lambda b, i: (b * nq + i, 0)),
            pl.BlockSpec((seq, 2 * IDX_DIM), lambda b, i: (b, 0)),
            pl.BlockSpec((seq, 2 * IDX_DIM), lambda b, i: (b, 0)),
            pl.BlockSpec((IDX_HEADS, tq), lambda b, i: (0, b * nq + i)),
        ],
        out_specs=pl.BlockSpec((1, seq, tq), lambda b, i: (b, 0, i)),
        out_shape=jax.ShapeDtypeStruct((bsz, seq, seq), BF16),
        scratch_shapes=[pltpu.VMEM((seq, tq), jnp.int32)],
        compiler_params=_params("parallel", "parallel"),
        name="dsa_select",
    )(qi, ke, ko, wit)


def _dsa_attn_kernel(q_ref, lat_ref, latt_ref, bias_ref, o_ref, m_ref, acc_ref,
                     s0_ref, s1_ref, p0_ref, p1_ref, a0_ref, a1_ref, *, hg, ng, tk):
    i = pl.program_id(1)
    tq = q_ref.shape[0]
    r = DSA_LATENT
    last_tile = lat_ref.shape[0] // tk - 1
    n_pairs = ((i + 1) * tq + 2 * tk - 1) // (2 * tk)
    tpos = i * tq + lax.broadcasted_iota(jnp.int32, (tk, tq), 1)
    row = lax.broadcasted_iota(jnp.int32, (tk, tq), 0)
    for g0 in range(0, DSA_HEADS // hg, ng):
        groups = [range((g0 + n) * hg, (g0 + n + 1) * hg) for n in range(ng)]
        q4 = [jnp.concatenate([q_ref[:, h * r:(h + 1) * r] for h in heads], axis=0)
              for heads in groups]

        def scores(j, s_ref):
            k0 = pl.multiple_of(j * tk, tk)
            lat = lat_ref[pl.ds(k0, tk), :]
            for n in range(ng):
                s_ref[n] = _dot_nt(lat, q4[n])

        def softmax(j, s_ref, p_ref, a_ref):
            k0 = pl.multiple_of(j * tk, tk)
            dist = jnp.abs(tpos - (k0 + row)).astype(F32)
            bias = bias_ref[0, pl.ds(k0, tk), :].astype(F32)
            for n, heads in enumerate(groups):
                for idx, h in enumerate(heads):
                    cs = slice(idx * tq, (idx + 1) * tq)
                    slope = LOG2E * 2.0 ** (-8.0 * (h + 1) / DSA_HEADS)
                    s = s_ref[n, :, cs] + (bias - slope * dist)
                    m_old = m_ref[n, :, cs]
                    m_new = jnp.maximum(m_old, jnp.max(s, axis=0, keepdims=True))
                    m_ref[n, :, cs] = m_new
                    p_ref[n, :, cs] = jnp.exp2(s - m_new).astype(BF16)
                    a_ref[n, :, cs] = jnp.exp2(m_old - m_new)

        def values(j, p_ref, a_ref):
            k0 = pl.multiple_of(j * tk, tk)
            latt = jnp.concatenate([latt_ref[:, pl.ds(k0, tk)], jnp.ones((ONES_ROWS, tk), BF16)], axis=0)
            pvs = [_dot(latt, p_ref[n]) for n in range(ng)]
            for n in range(ng):
                acc_ref[n] = a_ref[n] * acc_ref[n] + pvs[n]

        m_ref[...] = jnp.full(m_ref.shape, NEG_BIG, F32)
        acc_ref[...] = jnp.zeros(acc_ref.shape, F32)
        scores(0, s0_ref)

        def trip(ta, first):
            scores(ta + 1, s1_ref)
            if not first:
                values(ta - 1, p1_ref, a1_ref)
            softmax(ta, s0_ref, p0_ref, a0_ref)
            scores(jnp.minimum(ta + 2, last_tile), s0_ref)
            values(ta, p0_ref, a0_ref)
            softmax(ta + 1, s1_ref, p1_ref, a1_ref)

        def body(jj, carry):
            trip(2 * jj, False)
            return carry

        trip(0, True)
        lax.fori_loop(1, n_pairs, body, 0)
        values(2 * n_pairs - 1, p1_ref, a1_ref)
        for n, heads in enumerate(groups):
            out = (acc_ref[n, :r, :] / acc_ref[n, r:r + 1, :]).T
            for idx, h in enumerate(heads):
                o_ref[:, h * r:(h + 1) * r] = out[idx * tq:(idx + 1) * tq].astype(BF16)


def _dsa_attn(q, lat, latt, bias, bsz, seq, tq=128, hg=4, ng=2, tk=256):
    nq = seq // tq
    width = DSA_HEADS * DSA_LATENT
    cols = hg * tq
    assert (seq // tk) % 2 == 0
    return pl.pallas_call(
        functools.partial(_dsa_attn_kernel, hg=hg, ng=ng, tk=tk),
        grid=(bsz, nq),
        in_specs=[
            pl.BlockSpec((tq, width), lambda b, i: (b * nq + i, 0)),
            pl.BlockSpec((seq, DSA_LATENT), lambda b, i: (b, 0)),
            pl.BlockSpec((DSA_LATENT, seq), lambda b, i: (0, b)),
            pl.BlockSpec((1, seq, tq), lambda b, i: (b, 0, i)),
        ],
        out_specs=pl.BlockSpec((tq, width), lambda b, i: (b * nq + i, 0)),
        out_shape=jax.ShapeDtypeStruct((bsz * seq, width), BF16),
        scratch_shapes=[
            pltpu.VMEM((ng, 1, cols), F32),
            pltpu.VMEM((ng, DSA_LATENT + ONES_ROWS, cols), F32),
            pltpu.VMEM((ng, tk, cols), F32),
            pltpu.VMEM((ng, tk, cols), F32),
            pltpu.VMEM((ng, tk, cols), BF16),
            pltpu.VMEM((ng, tk, cols), BF16),
            pltpu.VMEM((ng, 1, cols), F32),
            pltpu.VMEM((ng, 1, cols), F32),
        ],
        compiler_params=_params("parallel", "parallel"),
        name="dsa_attn",
    )(q, lat, latt, bias)


def _split3(x):
    hi = x.astype(BF16)
    r1 = x - hi.astype(F32)
    mid = r1.astype(BF16)
    lo = (r1 - mid.astype(F32)).astype(BF16)
    return hi, mid, lo


def _fox_cum_kernel(h_ref, wt_ref, b_ref, cum_ref, *, blk):
    fl = _dot_nt(wt_ref[...], h_ref[...]) + b_ref[...]
    logf = jax.nn.log_sigmoid(fl)
    seq = logf.shape[1]
    r = lax.broadcasted_iota(jnp.int32, (blk, blk), 0)
    c = lax.broadcasted_iota(jnp.int32, (blk, blk), 1)
    upper = jnp.where(r <= c, 1.0, 0.0).astype(BF16)
    carry = jnp.zeros((logf.shape[0], 1), F32)
    for j in range(seq // blk):
        hi, mid, lo = _split3(logf[:, j * blk:(j + 1) * blk])
        cs = (_dot(hi, upper) + _dot(mid, upper)) + _dot(lo, upper) + carry
        cum_ref[0, :, j * blk:(j + 1) * blk] = cs
        carry = cs[:, blk - 1:blk]


def _fox_cum(h, w_fl_t, b_f, bsz, seq, blk=256):
    d = h.shape[1]
    return pl.pallas_call(
        functools.partial(_fox_cum_kernel, blk=blk),
        grid=(bsz,),
        in_specs=[
            pl.BlockSpec((seq, d), lambda b: (b, 0)),
            pl.BlockSpec((FOX_HEADS, d), lambda b: (0, 0)),
            pl.BlockSpec((FOX_HEADS, 1), lambda b: (0, 0)),
        ],
        out_specs=pl.BlockSpec((1, FOX_HEADS, seq), lambda b: (b, 0, 0)),
        out_shape=jax.ShapeDtypeStruct((bsz, FOX_HEADS, seq), F32),
        compiler_params=_params("parallel"),
        name="fox_cum",
    )(h, w_fl_t, b_f.reshape(FOX_HEADS, 1))


def _fox_attn_kernel(q_ref, k_ref, vt_ref, g_ref, o_ref, m_ref, acc_ref):
    i = pl.program_id(2)
    tq = q_ref.shape[0]
    hb = g_ref.shape[1] // HEAD_DIM
    qs = [q_ref[:, h * FOX_AUG:(h + 1) * FOX_AUG] for h in range(hb)]
    m_ref[...] = jnp.full(m_ref.shape, NEG_BIG, F32)
    acc_ref[...] = jnp.zeros(acc_ref.shape, F32)
    ones = jnp.ones((ONES_ROWS, tq), BF16)

    def tile(j, diagonal):
        k0 = pl.multiple_of(j * tq, tq)
        ss = [_dot_nt(k_ref[pl.ds(k0, tq), h * FOX_AUG:(h + 1) * FOX_AUG], qs[h])
              for h in range(hb)]
        ps, alphas = [], []
        for h in range(hb):
            s = ss[h]
            if diagonal:
                key = lax.broadcasted_iota(jnp.int32, (tq, tq), 0)
                qry = lax.broadcasted_iota(jnp.int32, (tq, tq), 1)
                s = jnp.where(key <= qry, s, NEG_BIG)
            m_old = m_ref[h]
            m_new = jnp.maximum(m_old, jnp.max(s, axis=0, keepdims=True))
            m_ref[h] = m_new
            ps.append(jnp.exp2(s - m_new).astype(BF16))
            alphas.append(jnp.exp2(m_old - m_new))
        pvs = [_dot(jnp.concatenate([vt_ref[h * HEAD_DIM:(h + 1) * HEAD_DIM, pl.ds(k0, tq)], ones], axis=0),
                    ps[h]) for h in range(hb)]
        for h in range(hb):
            acc_ref[h] = alphas[h] * acc_ref[h] + pvs[h]

    def body(j, carry):
        tile(j, False)
        return carry

    lax.fori_loop(0, i, body, 0)
    tile(i, True)
    for h in range(hb):
        cols = slice(h * HEAD_DIM, (h + 1) * HEAD_DIM)
        o = (acc_ref[h, :HEAD_DIM, :] / acc_ref[h, HEAD_DIM:HEAD_DIM + 1, :]).T
        o_ref[:, cols] = (o * g_ref[:, cols].astype(F32)).astype(BF16)


def _fox_attn(q, k, vt, gate, bsz, seq, tq=256, hb=4):
    nq = seq // tq
    d = gate.shape[1]
    qmap = lambda b, h, i: (b * nq + i, h)
    return pl.pallas_call(
        _fox_attn_kernel,
        grid=(bsz, FOX_HEADS // hb, nq),
        in_specs=[
            pl.BlockSpec((tq, hb * FOX_AUG), qmap),
            pl.BlockSpec((seq, hb * FOX_AUG), lambda b, h, i: (b, h)),
            pl.BlockSpec((hb * HEAD_DIM, seq), lambda b, h, i: (h, b)),
            pl.BlockSpec((tq, hb * HEAD_DIM), qmap),
        ],
        out_specs=pl.BlockSpec((tq, hb * HEAD_DIM), qmap),
        out_shape=jax.ShapeDtypeStruct((bsz * seq, d), BF16),
        scratch_shapes=[
            pltpu.VMEM((hb, 1, tq), F32),
            pltpu.VMEM((hb, HEAD_DIM + ONES_ROWS, tq), F32),
        ],
        compiler_params=_params("parallel", "parallel", "parallel"),
        name="fox_attn",
    )(q, k, vt, gate)


HGRN_SUB = 8


def _hgrn_kernel(q_ref, lg_ref, v_ref, gate_ref, gain_ref, o_ref,
                 state_ref, oacc_ref, kpad_ref, fpad_ref, vpad_ref, *, hb):
    c = pl.program_id(2)
    cl = q_ref.shape[0]
    dk = HEAD_DIM
    sub = HGRN_SUB

    @pl.when(c == 0)
    def _():
        state_ref[...] = jnp.zeros_like(state_ref)
        kpad_ref[0:sub, :] = jnp.zeros((sub, dk), F32)
        fpad_ref[0:sub, :] = jnp.zeros((sub, dk), F32)
        vpad_ref[0:sub, :] = jnp.zeros((sub, dk), F32)

    r = lax.broadcasted_iota(jnp.int32, (cl, cl), 0)
    cc = lax.broadcasted_iota(jnp.int32, (cl, cl), 1)
    lower = jnp.where(cc <= r, 1.0, 0.0).astype(BF16)
    block_start = lax.broadcasted_iota(jnp.int32, (cl, dk), 0) % sub == 0

    for hh in range(hb):
        cols = slice(hh * dk, (hh + 1) * dk)
        q = q_ref[:, cols].astype(F32)
        lg = lg_ref[:, cols]
        vb = v_ref[:, cols]
        v = vb.astype(F32)
        hi, mid, lo = _split3(lg)
        gcum = (_dot(lower, hi) + _dot(lower, mid)) + _dot(lower, lo)
        f = jnp.exp(lg)
        k = 1.0 - f
        st = state_ref[hh]

        oacc_ref[...] = _dot_nt((q * jnp.exp(gcum)).astype(BF16), st.astype(BF16))

        m = cl // 2
        while m >= sub:
            for blk in range(cl // (2 * m)):
                r0 = blk * 2 * m
                mid_row = r0 + m
                gm = gcum[mid_row - 1:mid_row, :]
                qs = (q[mid_row:mid_row + m] * jnp.exp(gcum[mid_row:mid_row + m] - gm)).astype(BF16)
                ks = (k[r0:mid_row] * jnp.exp(gm - gcum[r0:mid_row])).astype(BF16)
                sc = _dot_nt(qs, ks)
                oacc_ref[mid_row:mid_row + m, :] += _dot(sc.astype(BF16), vb[r0:mid_row])
            m //= 2

        kpad_ref[sub:, :] = k
        fpad_ref[sub:, :] = jnp.where(block_start, 0.0, f)
        vpad_ref[sub:, :] = v
        diag = jnp.sum(q * k, axis=-1, keepdims=True) * v
        dec = None
        for delta in range(1, sub):
            fd = fpad_ref[sub - delta + 1:sub - delta + 1 + cl, :]
            dec = fd if dec is None else dec * fd
            kd = kpad_ref[sub - delta:sub - delta + cl, :]
            vd = vpad_ref[sub - delta:sub - delta + cl, :]
            diag = diag + jnp.sum(q * kd * dec, axis=-1, keepdims=True) * vd
        o = oacc_ref[...] + diag

        gl = gcum[cl - 1:cl, :]
        kdec = (k * jnp.exp(gl - gcum)).astype(BF16)
        state_ref[hh] = st * jnp.exp(gl) + _dot_tn(vb, kdec)

        ms = jnp.mean(o * o, axis=-1, keepdims=True)
        y = o * lax.rsqrt(ms + EPS) * gain_ref[...]
        o_ref[:, cols] = (y * gate_ref[:, cols].astype(F32)).astype(BF16)


def _hgrn_core(q, lg, v, gate, gain, bsz, seq, cl=256, hb=4):
    nc = seq // cl
    d = q.shape[1]
    blk = lambda b, g, c: (b * nc + c, g)
    spec = pl.BlockSpec((cl, hb * HEAD_DIM), blk)
    return pl.pallas_call(
        functools.partial(_hgrn_kernel, hb=hb),
        grid=(bsz, HGRN_HEADS // hb, nc),
        in_specs=[spec, spec, spec, spec, pl.BlockSpec((1, HEAD_DIM), lambda b, g, c: (0, 0))],
        out_specs=spec,
        out_shape=jax.ShapeDtypeStruct((bsz * seq, d), BF16),
        scratch_shapes=[
            pltpu.VMEM((hb, HEAD_DIM, HEAD_DIM), F32),
            pltpu.VMEM((cl, HEAD_DIM), F32),
            pltpu.VMEM((cl + HGRN_SUB, HEAD_DIM), F32),
            pltpu.VMEM((cl + HGRN_SUB, HEAD_DIM), F32),
            pltpu.VMEM((cl + HGRN_SUB, HEAD_DIM), F32),
        ],
        compiler_params=_params("parallel", "parallel", "arbitrary"),
        name="hgrn_core",
    )(q, lg, v, gate, gain.reshape(1, HEAD_DIM))


def _dsa_mixer(h, w_in, q_gain, kv_gain, bsz, seq):
    d = h.shape[1]
    nq = DSA_HEADS * DSA_LATENT
    lat_w = w_in[:, nq:nq + DSA_LATENT]
    qi_w = w_in[:, nq + DSA_LATENT:nq + DSA_LATENT + IDX_HEADS * IDX_DIM]
    c0 = nq + DSA_LATENT + IDX_HEADS * IDX_DIM
    ki_w = w_in[:, c0:c0 + IDX_DIM]
    wi_w = w_in[:, c0 + IDX_DIM:c0 + IDX_DIM + IDX_HEADS]
    z = jnp.zeros((d, IDX_DIM), F32)
    w_s = jnp.concatenate(
        [lat_w, qi_w, ki_w, z, z, ki_w, wi_w, jnp.zeros((d, 128 - IDX_HEADS), F32)], axis=1).astype(BF16)
    q_gain_row = jnp.tile(q_gain, DSA_HEADS).reshape(1, nq)
    kv_gain_row = jnp.zeros((1, _DSA_S_N), F32).at[0, :DSA_LATENT].set(kv_gain)

    (q,) = _proj(h, w_in, 0, nq, _ep_dsa_q, [(1024, nq, BF16)], extras=(q_gain_row,))
    lat, qi, ke, ko, wi = _proj(
        h, w_s, 0, _DSA_S_N, _ep_dsa_small,
        [(DSA_LATENT, DSA_LATENT, BF16), (IDX_HEADS * IDX_DIM, IDX_HEADS * IDX_DIM, BF16),
         (2 * IDX_DIM, 2 * IDX_DIM, BF16), (2 * IDX_DIM, 2 * IDX_DIM, BF16), (IDX_HEADS, IDX_HEADS, F32)],
        extras=(kv_gain_row,), tn=_DSA_S_N)
    bias = _dsa_select(qi, ke, ko, wi.T, bsz, seq)
    return _dsa_attn(q, lat, lat.T, bias, bsz, seq)


def _fox_mixer(h, w_in, b_f, q_gain, k_gain, bsz, seq):
    d = h.shape[1]
    w = w_in
    w_fl_t = w_in[:, 4 * d:].T.astype(BF16)
    qg = jnp.tile(q_gain, FOX_HEADS).reshape(1, d)
    kg = jnp.tile(k_gain, FOX_HEADS).reshape(1, d)
    cum = _fox_cum(h, w_fl_t, b_f, bsz, seq)
    cum_t = cum.transpose(0, 2, 1).reshape(bsz * seq, FOX_HEADS)
    aug = [(1024 // HEAD_DIM * FOX_AUG, FOX_HEADS * FOX_AUG, BF16)]
    (q,) = _proj(h, w, 0, d, _ep_fox_q, aug, extras=(qg,))
    (k,) = _proj(h, w, d, d, _ep_fox_k, aug, extras=(kg,), row_extras=(cum_t,))
    vt = _proj_t(h, w_in[:, 2 * d:3 * d].T.astype(BF16))
    (gate,) = _proj(h, w, 3 * d, d, _ep_sigmoid, [(1024, d, BF16)])
    return _fox_attn(q, k, vt, gate, bsz, seq)


def _hgrn_mixer(h, w_in, lb, o_gain, bsz, seq):
    d = h.shape[1]
    w = w_in
    (q,) = _proj(h, w, 0, d, _ep_hgrn_q, [(1024, d, BF16)])
    (lg,) = _proj(h, w, d, d, _ep_hgrn_logf, [(1024, d, F32)], extras=(lb.reshape(1, d),))
    (v,) = _proj(h, w, 2 * d, d, _ep_cast, [(1024, d, BF16)])
    (gate,) = _proj(h, w, 3 * d, d, _ep_silu, [(1024, d, BF16)])
    return _hgrn_core(q, lg, v, gate, o_gain, bsz, seq)


def kernel(x, c, ada_w, ada_b, norm_mix_g, norm_ffn_g, dsa_w_in, dsa_q_norm, dsa_kv_norm, dsa_w_out, fox_w_in, fox_b_f, fox_q_norm, fox_k_norm, fox_w_out, hgrn_w_in, hgrn_lb, hgrn_o_norm, hgrn_w_out, ffn_w_in, ffn_conv_w, ffn_conv_b, ffn_w_out):
    bsz, seq, d = x.shape
    depth = ada_w.shape[0]
    f = ffn_conv_b.shape[1]

    mod = _ada_all(c, ada_w, ada_b)
    lb_soft = jax.nn.softmax(hgrn_lb.astype(F32), axis=0)
    lb_all = jnp.cumsum(lb_soft, axis=0) - lb_soft[0]
    conv_b = ffn_conv_b.reshape(depth, 1, f)

    x2 = x.reshape(bsz * seq, d)
    for i in range(depth):
        sh1, sc1, g1, sh2, sc2, g2 = [mod[i, :, j * d:(j + 1) * d] for j in range(6)]
        h = _norm_mod(x2, norm_mix_g[i], sh1, sc1, seq)
        kind, j = i % 3, i // 3
        if kind == 0:
            a = _dsa_mixer(h, dsa_w_in[j], dsa_q_norm[j], dsa_kv_norm[j], bsz, seq)
            w_out = dsa_w_out[j]
        elif kind == 1:
            a = _fox_mixer(h, fox_w_in[j], fox_b_f[j], fox_q_norm[j], fox_k_norm[j], bsz, seq)
            w_out = fox_w_out[j]
        else:
            a = _hgrn_mixer(h, hgrn_w_in[j], lb_all[i], hgrn_o_norm[j], bsz, seq)
            w_out = hgrn_w_out[j]
        x2 = _out_proj(a, w_out.astype(BF16), x2, g1, seq, tm=1024)
        h = _norm_mod(x2, norm_ffn_g[i], sh2, sc2, seq)
        a = _ffn_in(h, ffn_w_in, ffn_conv_w, conv_b, i, seq)
        x2 = _out_proj(a, ffn_w_out[i].astype(BF16), x2, g2, seq, tm=512)
    return x2.reshape(bsz, seq, d)
```

```python
import functools

import jax
import jax.numpy as jnp
from jax import lax
from jax.experimental import pallas as pl
from jax.experimental.pallas import tpu as pltpu

F32 = jnp.float32
BF16 = jnp.bfloat16

EPS = 1e-6
CHUNK = 64
TOPK_MAX = 256

DSA_HEADS = 16
DSA_LATENT = 256
IDX_HEADS = 16
IDX_DIM = 64
FOX_HEADS = 16
HGRN_HEADS = 16
HEAD_DIM = 128
CONV_WIDTH = 3

NEG_BIG = -1e30
LOG2E = 1.4426950408889634
ONES_ROWS = 16
INT_MIN = -2 ** 31

V7X_VMEM_BYTES = 64 * 1024 * 1024
VMEM_LIMIT = 56 * 1024 * 1024


def _params(*sem):
    return pltpu.CompilerParams(dimension_semantics=sem, vmem_limit_bytes=VMEM_LIMIT)


def _dot(a, b):
    return jnp.dot(a, b, preferred_element_type=F32)


def _dot_nt(a, b):
    return lax.dot_general(a, b, (((1,), (1,)), ((), ())), preferred_element_type=F32)


def _dot_tn(a, b):
    return lax.dot_general(a, b, (((0,), (0,)), ((), ())), preferred_element_type=F32)


def _silu(x):
    return x * jax.nn.sigmoid(x)


def _ada_kernel(c_ref, w_ref, b_ref, o_ref):
    cond = _silu(c_ref[...]).astype(BF16)
    o_ref[0] = _dot(cond, w_ref[0].astype(BF16)) + b_ref[0]


def _ada_all(c, ada_w, ada_b, tn=1024):
    depth, d, n = ada_w.shape
    bsz = c.shape[0]
    rows = 8
    c_pad = jnp.zeros((rows, d), F32).at[:bsz].set(c)
    out = pl.pallas_call(
        _ada_kernel,
        grid=(depth, n // tn),
        in_specs=[
            pl.BlockSpec((rows, d), lambda l, j: (0, 0)),
            pl.BlockSpec((1, d, tn), lambda l, j: (l, 0, j)),
            pl.BlockSpec((1, 1, tn), lambda l, j: (l, 0, j)),
        ],
        out_specs=pl.BlockSpec((1, rows, tn), lambda l, j: (l, 0, j)),
        out_shape=jax.ShapeDtypeStruct((depth, rows, n), F32),
        compiler_params=_params("parallel", "parallel"),
        name="ada_mod",
    )(c_pad, ada_w, ada_b.reshape(depth, 1, n))
    return out[:, :bsz]


def _norm_mod_kernel(x_ref, g_ref, sh_ref, sc_ref, h_ref):
    x = x_ref[...]
    ms = jnp.mean(x * x, axis=-1, keepdims=True)
    y = x * lax.rsqrt(ms + EPS) * g_ref[...]
    h_ref[...] = (y * (1.0 + sc_ref[0]) + sh_ref[0]).astype(BF16)


def _norm_mod(x2, g, shift, scale, seq, tm=512):
    t, d = x2.shape
    per_seq = seq // tm
    bsz = shift.shape[0]
    vec = lambda m: (m // per_seq, 0, 0)
    return pl.pallas_call(
        _norm_mod_kernel,
        grid=(t // tm,),
        in_specs=[
            pl.BlockSpec((tm, d), lambda m: (m, 0)),
            pl.BlockSpec((1, d), lambda m: (0, 0)),
            pl.BlockSpec((1, 1, d), vec),
            pl.BlockSpec((1, 1, d), vec),
        ],
        out_specs=pl.BlockSpec((tm, d), lambda m: (m, 0)),
        out_shape=jax.ShapeDtypeStruct((t, d), BF16),
        compiler_params=_params("parallel"),
        name="norm_mod",
    )(x2, g.reshape(1, d), shift.reshape(bsz, 1, d), scale.reshape(bsz, 1, d))


def _proj_kernel(*refs, epilogue, n_extra, cast_w):
    h_ref, w_ref = refs[0], refs[1]
    extras = refs[2:2 + n_extra]
    if cast_w:
        outs, wb_ref = refs[2 + n_extra:-1], refs[-1]

        @pl.when(pl.program_id(1) == 0)
        def _():
            wb_ref[...] = w_ref[...].astype(BF16)

        w = wb_ref[...]
    else:
        outs = refs[2 + n_extra:]
        w = w_ref[...]
    acc = _dot(h_ref[...], w)
    epilogue(acc, [e[...] for e in extras], outs)


def _proj(h, w, col0, n, epilogue, out_defs, extras=(), row_extras=(), tm=1024, tn=1024):
    t, k = h.shape
    tn = min(tn, n)
    tm = min(tm, t)
    assert n % tn == 0 and col0 % tn == 0 and t % tm == 0
    off = col0 // tn
    in_specs = [
        pl.BlockSpec((tm, k), lambda j, m: (m, 0)),
        pl.BlockSpec((k, tn), lambda j, m: (0, j + off)),
    ] + [pl.BlockSpec((1, tn), lambda j, m: (0, j)) for _ in extras
         ] + [pl.BlockSpec((tm, r.shape[1]), lambda j, m: (m, 0)) for r in row_extras]
    extras = tuple(extras) + tuple(row_extras)
    out_specs = [pl.BlockSpec((tm, wt), lambda j, m: (m, j)) for wt, _, _ in out_defs]
    out_shape = [jax.ShapeDtypeStruct((t, wtot), dt) for _, wtot, dt in out_defs]
    cast_w = w.dtype != BF16
    return pl.pallas_call(
        functools.partial(_proj_kernel, epilogue=epilogue, n_extra=len(extras), cast_w=cast_w),
        grid=(n // tn, t // tm),
        in_specs=in_specs,
        out_specs=out_specs,
        out_shape=out_shape,
        scratch_shapes=[pltpu.VMEM((k, tn), BF16)] if cast_w else [],
        compiler_params=_params("parallel", "arbitrary"),
        name="proj_" + epilogue.__name__.strip("_"),
    )(h, w, *extras)


def _proj_t_kernel(wt_ref, h_ref, o_ref):
    o_ref[...] = _dot_nt(wt_ref[...], h_ref[...]).astype(o_ref.dtype)


def _proj_t(h, wt, tm=1024, tn=1024):
    t, k = h.shape
    n = wt.shape[0]
    tm = min(tm, t)
    return pl.pallas_call(
        _proj_t_kernel,
        grid=(n // tn, t // tm),
        in_specs=[
            pl.BlockSpec((tn, k), lambda j, m: (j, 0)),
            pl.BlockSpec((tm, k), lambda j, m: (m, 0)),
        ],
        out_specs=pl.BlockSpec((tn, tm), lambda j, m: (j, m)),
        out_shape=jax.ShapeDtypeStruct((n, t), BF16),
        compiler_params=_params("parallel", "parallel"),
        name="proj_t",
    )(wt, h)


def _rms_heads(acc, gain_row, hd, scale, o_ref):
    for j in range(acc.shape[1] // hd):
        seg = acc[:, j * hd:(j + 1) * hd]
        ms = jnp.mean(seg * seg, axis=-1, keepdims=True)
        g = gain_row[:, j * hd:(j + 1) * hd]
        if scale != 1.0:
            g = g * scale
        o_ref[:, j * hd:(j + 1) * hd] = (seg * lax.rsqrt(ms + EPS) * g).astype(o_ref.dtype)


def _ep_dsa_q(acc, extras, outs):
    _rms_heads(acc, extras[0], DSA_LATENT, DSA_LATENT ** -0.5 * LOG2E, outs[0])


FOX_AUG = 2 * HEAD_DIM
FOX_BIAS_LANES = 3


def _rms_heads_aug(acc, gain_row, scale, extra_fn, o_ref):
    for j in range(acc.shape[1] // HEAD_DIM):
        cols = slice(j * HEAD_DIM, (j + 1) * HEAD_DIM)
        seg = acc[:, cols]
        ms = jnp.mean(seg * seg, axis=-1, keepdims=True)
        y = seg * lax.rsqrt(ms + EPS) * (gain_row[:, cols] * scale)
        o_ref[:, j * FOX_AUG:j * FOX_AUG + HEAD_DIM] = y.astype(o_ref.dtype)
        o_ref[:, j * FOX_AUG + HEAD_DIM:(j + 1) * FOX_AUG] = extra_fn(j).astype(o_ref.dtype)


def _ep_fox_q(acc, extras, outs):
    lane = lax.broadcasted_iota(jnp.int32, (acc.shape[0], HEAD_DIM), 1)
    ones = jnp.where(lane < FOX_BIAS_LANES, 1.0, 0.0)
    _rms_heads_aug(acc, extras[0], HEAD_DIM ** -0.5 * LOG2E, lambda j: ones, outs[0])


def _ep_fox_k(acc, extras, outs):
    gain_row, cum_t = extras
    heads = acc.shape[1] // HEAD_DIM
    first = pl.program_id(0) * heads
    r = lax.broadcasted_iota(jnp.int32, (FOX_HEADS, acc.shape[1]), 0)
    c = lax.broadcasted_iota(jnp.int32, (FOX_HEADS, acc.shape[1]), 1)
    head_col = jnp.where(r == first + c // HEAD_DIM, c % HEAD_DIM, -1)
    extra = jnp.zeros(acc.shape, F32)
    for i, term in enumerate(_split3(cum_t * (-LOG2E))):
        extra = extra + _dot(term, jnp.where(head_col == i, 1.0, 0.0).astype(BF16))
    _rms_heads_aug(acc, gain_row, 1.0,
                   lambda j: extra[:, j * HEAD_DIM:(j + 1) * HEAD_DIM], outs[0])


def _ep_cast(acc, extras, outs):
    outs[0][...] = acc.astype(outs[0].dtype)


def _ep_sigmoid(acc, extras, outs):
    outs[0][...] = jax.nn.sigmoid(acc).astype(outs[0].dtype)


def _ep_silu(acc, extras, outs):
    outs[0][...] = _silu(acc).astype(outs[0].dtype)


def _ep_hgrn_q(acc, extras, outs):
    outs[0][...] = (_silu(acc) * (HEAD_DIM ** -0.5)).astype(outs[0].dtype)


def _ep_hgrn_logf(acc, extras, outs):
    lb = extras[0]
    outs[0][...] = jnp.log(lb + (1.0 - lb) * jax.nn.sigmoid(acc))


_DSA_S_LAT = 0
_DSA_S_QI = DSA_LATENT
_DSA_S_KE = _DSA_S_QI + IDX_HEADS * IDX_DIM
_DSA_S_KO = _DSA_S_KE + 2 * IDX_DIM
_DSA_S_WI = _DSA_S_KO + 2 * IDX_DIM
_DSA_S_N = _DSA_S_WI + 128


def _ep_dsa_small(acc, extras, outs):
    lat_ref, qi_ref, ke_ref, ko_ref, wi_ref = outs
    _rms_heads(acc[:, :DSA_LATENT], extras[0][:, :DSA_LATENT], DSA_LATENT, 1.0, lat_ref)
    qi_ref[...] = acc[:, _DSA_S_QI:_DSA_S_KE].astype(BF16)
    ke_ref[...] = acc[:, _DSA_S_KE:_DSA_S_KO].astype(BF16)
    ko_ref[...] = acc[:, _DSA_S_KO:_DSA_S_WI].astype(BF16)
    wi_ref[...] = acc[:, _DSA_S_WI:_DSA_S_WI + IDX_HEADS] * (IDX_HEADS ** -0.5 * IDX_DIM ** -0.5)


def _out_kernel(a_ref, w_ref, x_ref, g_ref, o_ref):
    o_ref[...] = x_ref[...] + g_ref[0] * _dot(a_ref[...], w_ref[...])


def _out_proj(a, w, x2, gate, seq, tm, tn=1024):
    t, k = a.shape
    d = w.shape[1]
    bsz = gate.shape[0]
    per_seq = seq // tm
    return pl.pallas_call(
        _out_kernel,
        grid=(d // tn, t // tm),
        in_specs=[
            pl.BlockSpec((tm, k), lambda j, m: (m, 0)),
            pl.BlockSpec((k, tn), lambda j, m: (0, j)),
            pl.BlockSpec((tm, tn), lambda j, m: (m, j)),
            pl.BlockSpec((1, 1, tn), lambda j, m: (m // per_seq, 0, j)),
        ],
        out_specs=pl.BlockSpec((tm, tn), lambda j, m: (m, j)),
        out_shape=jax.ShapeDtypeStruct((t, d), F32),
        compiler_params=_params("parallel", "parallel"),
        name="out_proj",
    )(a, w, x2, gate.reshape(bsz, 1, d))


def _ffn_in_kernel(h_ref, wu_ref, wg_ref, cw_ref, cb_ref, a_ref, carry_ref, wub_ref, wgb_ref,
                   *, per_seq, rc, cc):
    m = pl.program_id(1)
    tm, tn = a_ref.shape

    @pl.when(m == 0)
    def _():
        wub_ref[...] = wu_ref[0].astype(BF16)
        wgb_ref[...] = wg_ref[0].astype(BF16)

    @pl.when(m % per_seq == 0)
    def _():
        carry_ref[...] = jnp.zeros_like(carry_ref)

    cw = cw_ref[0]
    cb = cb_ref[0]
    row = lax.broadcasted_iota(jnp.int32, (rc, cc), 0)
    for c in range(tn // cc):
        cols = slice(c * cc, (c + 1) * cc)
        prev = carry_ref[:, cols]
        for r in range(tm // rc):
            rows = slice(r * rc, (r + 1) * rc)
            h = h_ref[rows, :]
            u = _dot(h, wub_ref[:, cols])
            g = _dot(h, wgb_ref[:, cols])
            p1 = prev[7:8, :]
            p2 = prev[6:7, :]
            g1 = jnp.where(row == 0, p1, pltpu.roll(g, 1, axis=0))
            g2 = jnp.where(row == 0, p2, jnp.where(row == 1, p1, pltpu.roll(g, 2, axis=0)))
            conv = cw[0:1, cols] * g2 + cw[1:2, cols] * g1 + cw[2:3, cols] * g + cb[:, cols]
            a_ref[rows, cols] = (_silu(conv) * u).astype(BF16)
            prev = g[rc - 8:, :]
        carry_ref[:, cols] = prev


def _ffn_in(h, w_all, conv_w, conv_b, layer, seq, tm=2048, tn=512, rc=512, cc=256):
    t, k = h.shape
    f = w_all.shape[2] // 2
    nf = f // tn
    tm = min(tm, seq)
    per_seq = seq // tm
    rc = min(rc, tm)
    return pl.pallas_call(
        functools.partial(_ffn_in_kernel, per_seq=per_seq, rc=rc, cc=cc),
        grid=(nf, t // tm),
        in_specs=[
            pl.BlockSpec((tm, k), lambda j, m: (m, 0)),
            pl.BlockSpec((1, k, tn), lambda j, m: (layer, 0, j)),
            pl.BlockSpec((1, k, tn), lambda j, m: (layer, 0, j + nf)),
            pl.BlockSpec((1, CONV_WIDTH, tn), lambda j, m: (layer, 0, j)),
            pl.BlockSpec((1, 1, tn), lambda j, m: (layer, 0, j)),
        ],
        out_specs=pl.BlockSpec((tm, tn), lambda j, m: (m, j)),
        out_shape=jax.ShapeDtypeStruct((t, f), BF16),
        scratch_shapes=[
            pltpu.VMEM((8, tn), F32),
            pltpu.VMEM((k, tn), BF16),
            pltpu.VMEM((k, tn), BF16),
        ],
        compiler_params=_params("parallel", "arbitrary"),
        name="ffn_in",
    )(h, w_all, w_all, conv_w, conv_b)


def _dsa_select_kernel(qi_ref, ke_ref, ko_ref, wit_ref, bias_ref, key_ref, cut_ref, *, topk, ts):
    i = pl.program_id(1)
    tq = qi_ref.shape[0]
    seq = ke_ref.shape[0]
    n_tiles = ((i + 1) * tq + ts - 1) // ts
    qi = qi_ref[...]
    qit = jnp.concatenate([qi[:, 128 * j:128 * (j + 1)].astype(F32).T.astype(BF16)
                           for j in range(IDX_HEADS // 2)], axis=1)
    wit = wit_ref[...]
    tpos = i * tq + lax.broadcasted_iota(jnp.int32, (ts, tq), 1)
    first_hidden = (tpos // CHUNK + 1) * CHUNK
    row = lax.broadcasted_iota(jnp.int32, (ts, tq), 0)

    def score_tile(t, carry):
        k0 = pl.multiple_of(t * ts, ts)
        sce = _dot(ke_ref[pl.ds(k0, ts), :], qit)
        sco = _dot(ko_ref[pl.ds(k0, ts), :], qit)
        score = jnp.zeros((ts, tq), F32)
        for j in range(IDX_HEADS // 2):
            cols = slice(j * tq, (j + 1) * tq)
            score = score + wit[2 * j:2 * j + 1] * jnp.maximum(sce[:, cols], 0.0)
            score = score + wit[2 * j + 1:2 * j + 2] * jnp.maximum(sco[:, cols], 0.0)
        score = score + 0.0
        bits = pltpu.bitcast(score, jnp.int32)
        key = jnp.where(bits < 0, bits ^ jnp.int32(0x7FFFFFFF), bits)
        key_ref[pl.ds(k0, ts), :] = jnp.where(k0 + row < first_hidden, key, INT_MIN)
        return carry

    lax.fori_loop(0, n_tiles, score_tile, 0)
    kf = jnp.float32(topk)

    def count(hit_fn):
        def body(t, cnt):
            k0 = pl.multiple_of(t * ts, ts)
            return cnt + jnp.sum(hit_fn(key_ref[pl.ds(k0, ts), :], k0), axis=0, keepdims=True)
        return lax.fori_loop(0, n_tiles, body, jnp.zeros((1, tq), F32))

    def count_ge(cand):
        return count(lambda key, k0: jnp.where(key >= cand, 1.0, 0.0))

    visible = count_ge(jnp.full((1, tq), INT_MIN + 1, jnp.int32))
    c_pos = count_ge(jnp.zeros((1, tq), jnp.int32))
    thr0 = jnp.where(c_pos >= kf, 0, INT_MIN).astype(jnp.int32)
    cnt0 = jnp.where(c_pos >= kf, c_pos, jnp.float32(seq + 1))

    def bit_step(b, state):
        thr, cnt = state
        cand = thr + lax.shift_left(jnp.int32(1), jnp.int32(30) - b)
        c = count_ge(cand)
        take = c >= kf
        return jnp.where(take, cand, thr), jnp.where(take, c, cnt)

    thr, cnt = lax.fori_loop(0, 31, bit_step, (thr0, cnt0))

    cut_ref[...] = jnp.full((1, tq), seq, jnp.int32)
    tied = jnp.where(cnt > kf, jnp.where(visible > kf, 1.0, 0.0), 0.0)

    @pl.when(jnp.max(tied) > 0.0)
    def _():
        need = kf - count(lambda key, k0: jnp.where(key > thr, 1.0, 0.0))
        nbits = seq.bit_length()

        def idx_step(b, cut):
            cand = cut + lax.shift_left(jnp.int32(1), jnp.int32(nbits - 1) - b)
            c = count(lambda key, k0: jnp.where(key == thr, jnp.where(k0 + row < cand, 1.0, 0.0), 0.0))
            return jnp.where(c <= need, cand, cut)

        cut = lax.fori_loop(0, nbits, idx_step, jnp.zeros((1, tq), jnp.int32))
        cut_ref[...] = jnp.where(tied > 0.0, cut, seq)

    def write_tile(t, carry):
        k0 = pl.multiple_of(t * ts, ts)
        key = key_ref[pl.ds(k0, ts), :]
        tie_keep = jnp.where(k0 + row < cut_ref[...], 0.0, NEG_BIG)
        keep = jnp.where(key > thr, 0.0,
                         jnp.where(key == thr, jnp.where(key > INT_MIN, tie_keep, NEG_BIG), NEG_BIG))
        bias_ref[0, pl.ds(k0, ts), :] = keep.astype(BF16)
        return carry

    def fill_tile(t, carry):
        k0 = pl.multiple_of(t * ts, ts)
        bias_ref[0, pl.ds(k0, ts), :] = jnp.full((ts, tq), NEG_BIG, BF16)
        return carry

    lax.fori_loop(0, n_tiles, write_tile, 0)
    lax.fori_loop(n_tiles, seq // ts, fill_tile, 0)


def _dsa_select(qi, ke, ko, wit, bsz, seq, tq=256, ts=256):
    nq = seq // tq
    topk = min(TOPK_MAX, seq // 4)
    return pl.pallas_call(
        functools.partial(_dsa_select_kernel, topk=topk, ts=ts),
        grid=(bsz, nq),
        in_specs=[
            pl.BlockSpec((tq, IDX_HEADS * IDX_DIM), lambda b, i: (b * nq + i, 0)),
            pl.BlockSpec((seq, 2 * IDX_DIM), lambda b, i: (b, 0)),
            pl.BlockSpec((seq, 2 * IDX_DIM), lambda b, i: (b, 0)),
            pl.BlockSpec((IDX_HEADS, tq), lambda b, i: (0, b * nq + i)),
        ],
        out_specs=pl.BlockSpec((1, seq, tq), lambda b, i: (b, 0, i)),
        out_shape=jax.ShapeDtypeStruct((bsz, seq, seq), BF16),
        scratch_shapes=[pltpu.VMEM((seq, tq), jnp.int32), pltpu.VMEM((1, tq), jnp.int32)],
        compiler_params=_params("parallel", "parallel"),
        name="dsa_select",
    )(qi, ke, ko, wit)


def _dsa_attn_kernel(q_ref, lat_ref, latt_ref, bias_ref, o_ref, m_ref, acc_ref,
                     s0_ref, s1_ref, p0_ref, p1_ref, a0_ref, a1_ref, *, hg, ng, tk):
    i = pl.program_id(1)
    tq = q_ref.shape[0]
    r = DSA_LATENT
    last_tile = lat_ref.shape[0] // tk - 1
    n_pairs = ((i + 1) * tq + 2 * tk - 1) // (2 * tk)
    tpos = i * tq + lax.broadcasted_iota(jnp.int32, (tk, tq), 1)
    row = lax.broadcasted_iota(jnp.int32, (tk, tq), 0)
    for g0 in range(0, DSA_HEADS // hg, ng):
        groups = [range((g0 + n) * hg, (g0 + n + 1) * hg) for n in range(ng)]
        q4t = [jnp.concatenate([q_ref[:, h * r:(h + 1) * r].astype(F32).T.astype(BF16) for h in heads], axis=1)
               for heads in groups]

        def scores(j, s_ref):
            k0 = pl.multiple_of(j * tk, tk)
            lat = lat_ref[pl.ds(k0, tk), :]
            for n in range(ng):
                s_ref[n] = _dot(lat, q4t[n])

        def softmax(j, s_ref, p_ref, a_ref):
            k0 = pl.multiple_of(j * tk, tk)
            dist = jnp.abs(tpos - (k0 + row)).astype(F32)
            bias = bias_ref[0, pl.ds(k0, tk), :].astype(F32)
            for n, heads in enumerate(groups):
                for idx, h in enumerate(heads):
                    cs = slice(idx * tq, (idx + 1) * tq)
                    slope = LOG2E * 2.0 ** (-8.0 * (h + 1) / DSA_HEADS)
                    s = s_ref[n, :, cs] + (bias - slope * dist)
                    m_old = m_ref[n, :, cs]
                    m_new = jnp.maximum(m_old, jnp.max(s, axis=0, keepdims=True))
                    m_ref[n, :, cs] = m_new
                    p_ref[n, :, cs] = jnp.exp2(s - m_new).astype(BF16)
                    a_ref[n, :, cs] = jnp.exp2(m_old - m_new)

        def values(j, p_ref, a_ref):
            k0 = pl.multiple_of(j * tk, tk)
            latt = jnp.concatenate([latt_ref[:, pl.ds(k0, tk)], jnp.ones((ONES_ROWS, tk), BF16)], axis=0)
            pvs = [_dot(latt, p_ref[n]) for n in range(ng)]
            for n in range(ng):
                acc_ref[n] = a_ref[n] * acc_ref[n] + pvs[n]

        m_ref[...] = jnp.full(m_ref.shape, NEG_BIG, F32)
        acc_ref[...] = jnp.zeros(acc_ref.shape, F32)
        scores(0, s0_ref)

        def trip(ta, first):
            scores(ta + 1, s1_ref)
            if not first:
                values(ta - 1, p1_ref, a1_ref)
            softmax(ta, s0_ref, p0_ref, a0_ref)
            scores(jnp.minimum(ta + 2, last_tile), s0_ref)
            values(ta, p0_ref, a0_ref)
            softmax(ta + 1, s1_ref, p1_ref, a1_ref)

        def body(jj, carry):
            trip(2 * jj, False)
            return carry

        trip(0, True)
        lax.fori_loop(1, n_pairs, body, 0)
        values(2 * n_pairs - 1, p1_ref, a1_ref)
        for n, heads in enumerate(groups):
            out = (acc_ref[n, :r, :] / acc_ref[n, r:r + 1, :]).T
            for idx, h in enumerate(heads):
                o_ref[:, h * r:(h + 1) * r] = out[idx * tq:(idx + 1) * tq].astype(BF16)


def _dsa_attn(q, lat, latt, bias, bsz, seq, tq=128, hg=4, ng=2, tk=256):
    nq = seq // tq
    width = DSA_HEADS * DSA_LATENT
    cols = hg * tq
    assert (seq // tk) % 2 == 0
    return pl.pallas_call(
        functools.partial(_dsa_attn_kernel, hg=hg, ng=ng, tk=tk),
        grid=(bsz, nq),
        in_specs=[
            pl.BlockSpec((tq, width), lambda b, i: (b * nq + i, 0)),
            pl.BlockSpec((seq, DSA_LATENT), lambda b, i: (b, 0)),
            pl.BlockSpec((DSA_LATENT, seq), lambda b, i: (0, b)),
            pl.BlockSpec((1, seq, tq), lambda b, i: (b, 0, i)),
        ],
        out_specs=pl.BlockSpec((tq, width), lambda b, i: (b * nq + i, 0)),
        out_shape=jax.ShapeDtypeStruct((bsz * seq, width), BF16),
        scratch_shapes=[
            pltpu.VMEM((ng, 1, cols), F32),
            pltpu.VMEM((ng, DSA_LATENT + ONES_ROWS, cols), F32),
            pltpu.VMEM((ng, tk, cols), F32),
            pltpu.VMEM((ng, tk, cols), F32),
            pltpu.VMEM((ng, tk, cols), BF16),
            pltpu.VMEM((ng, tk, cols), BF16),
            pltpu.VMEM((ng, 1, cols), F32),
            pltpu.VMEM((ng, 1, cols), F32),
        ],
        compiler_params=_params("parallel", "parallel"),
        name="dsa_attn",
    )(q, lat, latt, bias)


def _split3(x):
    hi = x.astype(BF16)
    r1 = x - hi.astype(F32)
    mid = r1.astype(BF16)
    lo = (r1 - mid.astype(F32)).astype(BF16)
    return hi, mid, lo


def _fox_cum_kernel(h_ref, wt_ref, b_ref, cum_ref, *, blk):
    fl = _dot_nt(wt_ref[...], h_ref[...]) + b_ref[...]
    logf = jax.nn.log_sigmoid(fl)
    seq = logf.shape[1]
    r = lax.broadcasted_iota(jnp.int32, (blk, blk), 0)
    c = lax.broadcasted_iota(jnp.int32, (blk, blk), 1)
    upper = jnp.where(r <= c, 1.0, 0.0).astype(BF16)
    carry = jnp.zeros((logf.shape[0], 1), F32)
    for j in range(seq // blk):
        hi, mid, lo = _split3(logf[:, j * blk:(j + 1) * blk])
        cs = (_dot(hi, upper) + _dot(mid, upper)) + _dot(lo, upper) + carry
        cum_ref[0, :, j * blk:(j + 1) * blk] = cs
        carry = cs[:, blk - 1:blk]


def _fox_cum(h, w_fl_t, b_f, bsz, seq, blk=256):
    d = h.shape[1]
    return pl.pallas_call(
        functools.partial(_fox_cum_kernel, blk=blk),
        grid=(bsz,),
        in_specs=[
            pl.BlockSpec((seq, d), lambda b: (b, 0)),
            pl.BlockSpec((FOX_HEADS, d), lambda b: (0, 0)),
            pl.BlockSpec((FOX_HEADS, 1), lambda b: (0, 0)),
        ],
        out_specs=pl.BlockSpec((1, FOX_HEADS, seq), lambda b: (b, 0, 0)),
        out_shape=jax.ShapeDtypeStruct((bsz, FOX_HEADS, seq), F32),
        compiler_params=_params("parallel"),
        name="fox_cum",
    )(h, w_fl_t, b_f.reshape(FOX_HEADS, 1))


def _fox_attn_kernel(q_ref, k_ref, vt_ref, g_ref, o_ref, m_ref, acc_ref):
    i = pl.program_id(2)
    tq = q_ref.shape[0]
    hb = g_ref.shape[1] // HEAD_DIM
    qts = [q_ref[:, h * FOX_AUG:(h + 1) * FOX_AUG].astype(F32).T.astype(BF16) for h in range(hb)]
    m_ref[...] = jnp.full(m_ref.shape, NEG_BIG, F32)
    acc_ref[...] = jnp.zeros(acc_ref.shape, F32)
    ones = jnp.ones((ONES_ROWS, tq), BF16)

    def tile(j, diagonal):
        k0 = pl.multiple_of(j * tq, tq)
        ss = [_dot(k_ref[pl.ds(k0, tq), h * FOX_AUG:(h + 1) * FOX_AUG], qts[h])
              for h in range(hb)]
        ps, alphas = [], []
        for h in range(hb):
            s = ss[h]
            if diagonal:
                key = lax.broadcasted_iota(jnp.int32, (tq, tq), 0)
                qry = lax.broadcasted_iota(jnp.int32, (tq, tq), 1)
                s = jnp.where(key <= qry, s, NEG_BIG)
            m_old = m_ref[h]
            m_new = jnp.maximum(m_old, jnp.max(s, axis=0, keepdims=True))
            m_ref[h] = m_new
            ps.append(jnp.exp2(s - m_new).astype(BF16))
            alphas.append(jnp.exp2(m_old - m_new))
        pvs = [_dot(jnp.concatenate([vt_ref[h * HEAD_DIM:(h + 1) * HEAD_DIM, pl.ds(k0, tq)], ones], axis=0),
                    ps[h]) for h in range(hb)]
        for h in range(hb):
            acc_ref[h] = alphas[h] * acc_ref[h] + pvs[h]

    def body(j, carry):
        tile(j, False)
        return carry

    lax.fori_loop(0, i, body, 0)
    tile(i, True)
    for h in range(hb):
        cols = slice(h * HEAD_DIM, (h + 1) * HEAD_DIM)
        o = (acc_ref[h, :HEAD_DIM, :] / acc_ref[h, HEAD_DIM:HEAD_DIM + 1, :]).T
        o_ref[:, cols] = (o * g_ref[:, cols].astype(F32)).astype(BF16)


def _fox_attn(q, k, vt, gate, bsz, seq, tq=256, hb=4):
    nq = seq // tq
    d = gate.shape[1]
    qmap = lambda b, h, i: (b * nq + i, h)
    return pl.pallas_call(
        _fox_attn_kernel,
        grid=(bsz, FOX_HEADS // hb, nq),
        in_specs=[
            pl.BlockSpec((tq, hb * FOX_AUG), qmap),
            pl.BlockSpec((seq, hb * FOX_AUG), lambda b, h, i: (b, h)),
            pl.BlockSpec((hb * HEAD_DIM, seq), lambda b, h, i: (h, b)),
            pl.BlockSpec((tq, hb * HEAD_DIM), qmap),
        ],
        out_specs=pl.BlockSpec((tq, hb * HEAD_DIM), qmap),
        out_shape=jax.ShapeDtypeStruct((bsz * seq, d), BF16),
        scratch_shapes=[
            pltpu.VMEM((hb, 1, tq), F32),
            pltpu.VMEM((hb, HEAD_DIM + ONES_ROWS, tq), F32),
        ],
        compiler_params=_params("parallel", "parallel", "parallel"),
        name="fox_attn",
    )(q, k, vt, gate)


HGRN_SUB = 8


def _hgrn_kernel(q_ref, lg_ref, v_ref, gate_ref, gain_ref, o_ref,
                 state_ref, oacc_ref, kpad_ref, fpad_ref, vpad_ref, *, hb):
    c = pl.program_id(2)
    cl = q_ref.shape[0]
    dk = HEAD_DIM
    sub = HGRN_SUB

    @pl.when(c == 0)
    def _():
        state_ref[...] = jnp.zeros_like(state_ref)
        kpad_ref[0:sub, :] = jnp.zeros((sub, dk), F32)
        fpad_ref[0:sub, :] = jnp.zeros((sub, dk), F32)
        vpad_ref[0:sub, :] = jnp.zeros((sub, dk), F32)

    r = lax.broadcasted_iota(jnp.int32, (cl, cl), 0)
    cc = lax.broadcasted_iota(jnp.int32, (cl, cl), 1)
    lower = jnp.where(cc <= r, 1.0, 0.0).astype(BF16)
    block_start = lax.broadcasted_iota(jnp.int32, (cl, dk), 0) % sub == 0

    for hh in range(hb):
        cols = slice(hh * dk, (hh + 1) * dk)
        q = q_ref[:, cols].astype(F32)
        lg = lg_ref[:, cols]
        vb = v_ref[:, cols]
        v = vb.astype(F32)
        hi, mid, lo = _split3(lg)
        gcum = (_dot(lower, hi) + _dot(lower, mid)) + _dot(lower, lo)
        f = jnp.exp(lg)
        k = 1.0 - f
        st = state_ref[hh]

        oacc_ref[...] = _dot_nt((q * jnp.exp(gcum)).astype(BF16), st.astype(BF16))

        m = cl // 2
        while m >= sub:
            for blk in range(cl // (2 * m)):
                r0 = blk * 2 * m
                mid_row = r0 + m
                gm = gcum[mid_row - 1:mid_row, :]
                qs = (q[mid_row:mid_row + m] * jnp.exp(gcum[mid_row:mid_row + m] - gm)).astype(BF16)
                ks = (k[r0:mid_row] * jnp.exp(gm - gcum[r0:mid_row])).astype(BF16)
                sc = _dot_nt(qs, ks)
                oacc_ref[mid_row:mid_row + m, :] += _dot(sc.astype(BF16), vb[r0:mid_row])
            m //= 2

        kpad_ref[sub:, :] = k
        fpad_ref[sub:, :] = jnp.where(block_start, 0.0, f)
        vpad_ref[sub:, :] = v
        diag = jnp.sum(q * k, axis=-1, keepdims=True) * v
        dec = None
        for delta in range(1, sub):
            fd = fpad_ref[sub - delta + 1:sub - delta + 1 + cl, :]
            dec = fd if dec is None else dec * fd
            kd = kpad_ref[sub - delta:sub - delta + cl, :]
            vd = vpad_ref[sub - delta:sub - delta + cl, :]
            diag = diag + jnp.sum(q * kd * dec, axis=-1, keepdims=True) * vd
        o = oacc_ref[...] + diag

        gl = gcum[cl - 1:cl, :]
        kdec = (k * jnp.exp(gl - gcum)).astype(BF16)
        state_ref[hh] = st * jnp.exp(gl) + _dot_tn(vb, kdec)

        ms = jnp.mean(o * o, axis=-1, keepdims=True)
        y = o * lax.rsqrt(ms + EPS) * gain_ref[...]
        o_ref[:, cols] = (y * gate_ref[:, cols].astype(F32)).astype(BF16)


def _hgrn_core(q, lg, v, gate, gain, bsz, seq, cl=256, hb=4):
    nc = seq // cl
    d = q.shape[1]
    blk = lambda b, g, c: (b * nc + c, g)
    spec = pl.BlockSpec((cl, hb * HEAD_DIM), blk)
    return pl.pallas_call(
        functools.partial(_hgrn_kernel, hb=hb),
        grid=(bsz, HGRN_HEADS // hb, nc),
        in_specs=[spec, spec, spec, spec, pl.BlockSpec((1, HEAD_DIM), lambda b, g, c: (0, 0))],
        out_specs=spec,
        out_shape=jax.ShapeDtypeStruct((bsz * seq, d), BF16),
        scratch_shapes=[
            pltpu.VMEM((hb, HEAD_DIM, HEAD_DIM), F32),
            pltpu.VMEM((cl, HEAD_DIM), F32),
            pltpu.VMEM((cl + HGRN_SUB, HEAD_DIM), F32),
            pltpu.VMEM((cl + HGRN_SUB, HEAD_DIM), F32),
            pltpu.VMEM((cl + HGRN_SUB, HEAD_DIM), F32),
        ],
        compiler_params=_params("parallel", "parallel", "arbitrary"),
        name="hgrn_core",
    )(q, lg, v, gate, gain.reshape(1, HEAD_DIM))


def _dsa_mixer(h, w_in, q_gain, kv_gain, bsz, seq):
    d = h.shape[1]
    nq = DSA_HEADS * DSA_LATENT
    lat_w = w_in[:, nq:nq + DSA_LATENT]
    qi_w = w_in[:, nq + DSA_LATENT:nq + DSA_LATENT + IDX_HEADS * IDX_DIM]
    c0 = nq + DSA_LATENT + IDX_HEADS * IDX_DIM
    ki_w = w_in[:, c0:c0 + IDX_DIM]
    wi_w = w_in[:, c0 + IDX_DIM:c0 + IDX_DIM + IDX_HEADS]
    z = jnp.zeros((d, IDX_DIM), F32)
    w_s = jnp.concatenate(
        [lat_w, qi_w, ki_w, z, z, ki_w, wi_w, jnp.zeros((d, 128 - IDX_HEADS), F32)], axis=1).astype(BF16)
    q_gain_row = jnp.tile(q_gain, DSA_HEADS).reshape(1, nq)
    kv_gain_row = jnp.zeros((1, _DSA_S_N), F32).at[0, :DSA_LATENT].set(kv_gain)

    (q,) = _proj(h, w_in, 0, nq, _ep_dsa_q, [(1024, nq, BF16)], extras=(q_gain_row,))
    lat, qi, ke, ko, wi = _proj(
        h, w_s, 0, _DSA_S_N, _ep_dsa_small,
        [(DSA_LATENT, DSA_LATENT, BF16), (IDX_HEADS * IDX_DIM, IDX_HEADS * IDX_DIM, BF16),
         (2 * IDX_DIM, 2 * IDX_DIM, BF16), (2 * IDX_DIM, 2 * IDX_DIM, BF16), (IDX_HEADS, IDX_HEADS, F32)],
        extras=(kv_gain_row,), tn=_DSA_S_N)
    bias = _dsa_select(qi, ke, ko, wi.T, bsz, seq)
    return _dsa_attn(q, lat, lat.T, bias, bsz, seq)


def _fox_mixer(h, w_in, b_f, q_gain, k_gain, bsz, seq):
    d = h.shape[1]
    w = w_in
    w_fl_t = w_in[:, 4 * d:].T.astype(BF16)
    qg = jnp.tile(q_gain, FOX_HEADS).reshape(1, d)
    kg = jnp.tile(k_gain, FOX_HEADS).reshape(1, d)
    cum = _fox_cum(h, w_fl_t, b_f, bsz, seq)
    cum_t = cum.transpose(0, 2, 1).reshape(bsz * seq, FOX_HEADS)
    aug = [(1024 // HEAD_DIM * FOX_AUG, FOX_HEADS * FOX_AUG, BF16)]
    (q,) = _proj(h, w, 0, d, _ep_fox_q, aug, extras=(qg,))
    (k,) = _proj(h, w, d, d, _ep_fox_k, aug, extras=(kg,), row_extras=(cum_t,))
    vt = _proj_t(h, w_in[:, 2 * d:3 * d].T.astype(BF16))
    (gate,) = _proj(h, w, 3 * d, d, _ep_sigmoid, [(1024, d, BF16)])
    return _fox_attn(q, k, vt, gate, bsz, seq)


def _hgrn_mixer(h, w_in, lb, o_gain, bsz, seq):
    d = h.shape[1]
    w = w_in
    (q,) = _proj(h, w, 0, d, _ep_hgrn_q, [(1024, d, BF16)])
    (lg,) = _proj(h, w, d, d, _ep_hgrn_logf, [(1024, d, F32)], extras=(lb.reshape(1, d),))
    (v,) = _proj(h, w, 2 * d, d, _ep_cast, [(1024, d, BF16)])
    (gate,) = _proj(h, w, 3 * d, d, _ep_silu, [(1024, d, BF16)])
    return _hgrn_core(q, lg, v, gate, o_gain, bsz, seq)


def kernel(x, c, ada_w, ada_b, norm_mix_g, norm_ffn_g, dsa_w_in, dsa_q_norm, dsa_kv_norm, dsa_w_out, fox_w_in, fox_b_f, fox_q_norm, fox_k_norm, fox_w_out, hgrn_w_in, hgrn_lb, hgrn_o_norm, hgrn_w_out, ffn_w_in, ffn_conv_w, ffn_conv_b, ffn_w_out):
    bsz, seq, d = x.shape
    depth = ada_w.shape[0]
    f = ffn_conv_b.shape[1]

    mod = _ada_all(c, ada_w, ada_b)
    lb_soft = jax.nn.softmax(hgrn_lb.astype(F32), axis=0)
    lb_all = jnp.cumsum(lb_soft, axis=0) - lb_soft[0]
    conv_b = ffn_conv_b.reshape(depth, 1, f)

    x2 = x.reshape(bsz * seq, d)
    for i in range(depth):
        sh1, sc1, g1, sh2, sc2, g2 = [mod[i, :, j * d:(j + 1) * d] for j in range(6)]
        h = _norm_mod(x2, norm_mix_g[i], sh1, sc1, seq)
        kind, j = i % 3, i // 3
        if kind == 0:
            a = _dsa_mixer(h, dsa_w_in[j], dsa_q_norm[j], dsa_kv_norm[j], bsz, seq)
            w_out = dsa_w_out[j]
        elif kind == 1:
            a = _fox_mixer(h, fox_w_in[j], fox_b_f[j], fox_q_norm[j], fox_k_norm[j], bsz, seq)
            w_out = fox_w_out[j]
        else:
            a = _hgrn_mixer(h, hgrn_w_in[j], lb_all[i], hgrn_o_norm[j], bsz, seq)
            w_out = hgrn_w_out[j]
        x2 = _out_proj(a, w_out.astype(BF16), x2, g1, seq, tm=1024)
        h = _norm_mod(x2, norm_ffn_g[i], sh2, sc2, seq)
        a = _ffn_in(h, ffn_w_in, ffn_conv_w, conv_b, i, seq)
        x2 = _out_proj(a, ffn_w_out[i].astype(BF16), x2, g2, seq, tm=512)
    return x2.reshape(bsz, seq, d)
```

```python
import functools

import jax
import jax.numpy as jnp
from jax import lax
from jax.experimental import pallas as pl
from jax.experimental.pallas import tpu as pltpu

F32 = jnp.float32
BF16 = jnp.bfloat16

EPS = 1e-6
CHUNK = 64
TOPK_MAX = 256

DSA_HEADS = 16
DSA_LATENT = 256
IDX_HEADS = 16
IDX_DIM = 64
FOX_HEADS = 16
HGRN_HEADS = 16
HEAD_DIM = 128
CONV_WIDTH = 3

NEG_BIG = -1e30
LOG2E = 1.4426950408889634
ONES_ROWS = 16
INT_MIN = -2 ** 31

V7X_VMEM_BYTES = 64 * 1024 * 1024
VMEM_LIMIT = 56 * 1024 * 1024


def _params(*sem):
    return pltpu.CompilerParams(dimension_semantics=sem, vmem_limit_bytes=VMEM_LIMIT)


def _dot(a, b):
    return jnp.dot(a, b, preferred_element_type=F32)


def _dot_nt(a, b):
    return lax.dot_general(a, b, (((1,), (1,)), ((), ())), preferred_element_type=F32)


def _dot_tn(a, b):
    return lax.dot_general(a, b, (((0,), (0,)), ((), ())), preferred_element_type=F32)


def _silu(x):
    return x * jax.nn.sigmoid(x)


def _ada_kernel(c_ref, w_ref, b_ref, o_ref):
    cond = _silu(c_ref[...]).astype(BF16)
    o_ref[0] = _dot(cond, w_ref[0].astype(BF16)) + b_ref[0]


def _ada_all(c, ada_w, ada_b, tn=1024):
    depth, d, n = ada_w.shape
    bsz = c.shape[0]
    rows = 8
    c_pad = jnp.zeros((rows, d), F32).at[:bsz].set(c)
    out = pl.pallas_call(
        _ada_kernel,
        grid=(depth, n // tn),
        in_specs=[
            pl.BlockSpec((rows, d), lambda l, j: (0, 0)),
            pl.BlockSpec((1, d, tn), lambda l, j: (l, 0, j)),
            pl.BlockSpec((1, 1, tn), lambda l, j: (l, 0, j)),
        ],
        out_specs=pl.BlockSpec((1, rows, tn), lambda l, j: (l, 0, j)),
        out_shape=jax.ShapeDtypeStruct((depth, rows, n), F32),
        compiler_params=_params("parallel", "parallel"),
        name="ada_mod",
    )(c_pad, ada_w, ada_b.reshape(depth, 1, n))
    return out[:, :bsz]


def _norm_mod_kernel(x_ref, g_ref, sh_ref, sc_ref, h_ref):
    x = x_ref[...]
    ms = jnp.mean(x * x, axis=-1, keepdims=True)
    y = x * lax.rsqrt(ms + EPS) * g_ref[...]
    h_ref[...] = (y * (1.0 + sc_ref[0]) + sh_ref[0]).astype(BF16)


def _norm_mod(x2, g, shift, scale, seq, tm=512):
    t, d = x2.shape
    per_seq = seq // tm
    bsz = shift.shape[0]
    vec = lambda m: (m // per_seq, 0, 0)
    return pl.pallas_call(
        _norm_mod_kernel,
        grid=(t // tm,),
        in_specs=[
            pl.BlockSpec((tm, d), lambda m: (m, 0)),
            pl.BlockSpec((1, d), lambda m: (0, 0)),
            pl.BlockSpec((1, 1, d), vec),
            pl.BlockSpec((1, 1, d), vec),
        ],
        out_specs=pl.BlockSpec((tm, d), lambda m: (m, 0)),
        out_shape=jax.ShapeDtypeStruct((t, d), BF16),
        compiler_params=_params("parallel"),
        name="norm_mod",
    )(x2, g.reshape(1, d), shift.reshape(bsz, 1, d), scale.reshape(bsz, 1, d))


def _proj_kernel(*refs, epilogue, n_extra, cast_w):
    h_ref, w_ref = refs[0], refs[1]
    extras = refs[2:2 + n_extra]
    if cast_w:
        outs, wb_ref = refs[2 + n_extra:-1], refs[-1]

        @pl.when(pl.program_id(1) == 0)
        def _():
            wb_ref[...] = w_ref[...].astype(BF16)

        w = wb_ref[...]
    else:
        outs = refs[2 + n_extra:]
        w = w_ref[...]
    acc = _dot(h_ref[...], w)
    epilogue(acc, [e[...] for e in extras], outs)


def _proj(h, w, col0, n, epilogue, out_defs, extras=(), row_extras=(), tm=1024, tn=1024):
    t, k = h.shape
    tn = min(tn, n)
    tm = min(tm, t)
    assert n % tn == 0 and col0 % tn == 0 and t % tm == 0
    off = col0 // tn
    in_specs = [
        pl.BlockSpec((tm, k), lambda j, m: (m, 0)),
        pl.BlockSpec((k, tn), lambda j, m: (0, j + off)),
    ] + [pl.BlockSpec((1, tn), lambda j, m: (0, j)) for _ in extras
         ] + [pl.BlockSpec((tm, r.shape[1]), lambda j, m: (m, 0)) for r in row_extras]
    extras = tuple(extras) + tuple(row_extras)
    out_specs = [pl.BlockSpec((tm, wt), lambda j, m: (m, j)) for wt, _, _ in out_defs]
    out_shape = [jax.ShapeDtypeStruct((t, wtot), dt) for _, wtot, dt in out_defs]
    cast_w = w.dtype != BF16
    return pl.pallas_call(
        functools.partial(_proj_kernel, epilogue=epilogue, n_extra=len(extras), cast_w=cast_w),
        grid=(n // tn, t // tm),
        in_specs=in_specs,
        out_specs=out_specs,
        out_shape=out_shape,
        scratch_shapes=[pltpu.VMEM((k, tn), BF16)] if cast_w else [],
        compiler_params=_params("parallel", "arbitrary"),
        name="proj_" + epilogue.__name__.strip("_"),
    )(h, w, *extras)


def _proj_t_kernel(wt_ref, h_ref, o_ref):
    o_ref[...] = _dot_nt(wt_ref[...], h_ref[...]).astype(o_ref.dtype)


def _proj_t(h, wt, tm=1024, tn=1024):
    t, k = h.shape
    n = wt.shape[0]
    tm = min(tm, t)
    return pl.pallas_call(
        _proj_t_kernel,
        grid=(n // tn, t // tm),
        in_specs=[
            pl.BlockSpec((tn, k), lambda j, m: (j, 0)),
            pl.BlockSpec((tm, k), lambda j, m: (m, 0)),
        ],
        out_specs=pl.BlockSpec((tn, tm), lambda j, m: (j, m)),
        out_shape=jax.ShapeDtypeStruct((n, t), BF16),
        compiler_params=_params("parallel", "parallel"),
        name="proj_t",
    )(wt, h)


def _rms_heads(acc, gain_row, hd, scale, o_ref):
    for j in range(acc.shape[1] // hd):
        seg = acc[:, j * hd:(j + 1) * hd]
        ms = jnp.mean(seg * seg, axis=-1, keepdims=True)
        g = gain_row[:, j * hd:(j + 1) * hd]
        if scale != 1.0:
            g = g * scale
        o_ref[:, j * hd:(j + 1) * hd] = (seg * lax.rsqrt(ms + EPS) * g).astype(o_ref.dtype)


def _ep_dsa_q(acc, extras, outs):
    _rms_heads(acc, extras[0], DSA_LATENT, DSA_LATENT ** -0.5 * LOG2E, outs[0])


FOX_AUG = 2 * HEAD_DIM
FOX_BIAS_LANES = 3


def _rms_heads_aug(acc, gain_row, scale, extra_fn, o_ref):
    for j in range(acc.shape[1] // HEAD_DIM):
        cols = slice(j * HEAD_DIM, (j + 1) * HEAD_DIM)
        seg = acc[:, cols]
        ms = jnp.mean(seg * seg, axis=-1, keepdims=True)
        y = seg * lax.rsqrt(ms + EPS) * (gain_row[:, cols] * scale)
        o_ref[:, j * FOX_AUG:j * FOX_AUG + HEAD_DIM] = y.astype(o_ref.dtype)
        o_ref[:, j * FOX_AUG + HEAD_DIM:(j + 1) * FOX_AUG] = extra_fn(j).astype(o_ref.dtype)


def _ep_fox_q(acc, extras, outs):
    lane = lax.broadcasted_iota(jnp.int32, (acc.shape[0], HEAD_DIM), 1)
    ones = jnp.where(lane < FOX_BIAS_LANES, 1.0, 0.0)
    _rms_heads_aug(acc, extras[0], HEAD_DIM ** -0.5 * LOG2E, lambda j: ones, outs[0])


def _ep_fox_k(acc, extras, outs):
    gain_row, cum_t = extras
    heads = acc.shape[1] // HEAD_DIM
    first = pl.program_id(0) * heads
    r = lax.broadcasted_iota(jnp.int32, (FOX_HEADS, acc.shape[1]), 0)
    c = lax.broadcasted_iota(jnp.int32, (FOX_HEADS, acc.shape[1]), 1)
    head_col = jnp.where(r == first + c // HEAD_DIM, c % HEAD_DIM, -1)
    extra = jnp.zeros(acc.shape, F32)
    for i, term in enumerate(_split3(cum_t * (-LOG2E))):
        extra = extra + _dot(term, jnp.where(head_col == i, 1.0, 0.0).astype(BF16))
    _rms_heads_aug(acc, gain_row, 1.0,
                   lambda j: extra[:, j * HEAD_DIM:(j + 1) * HEAD_DIM], outs[0])


def _ep_cast(acc, extras, outs):
    outs[0][...] = acc.astype(outs[0].dtype)


def _ep_sigmoid(acc, extras, outs):
    outs[0][...] = jax.nn.sigmoid(acc).astype(outs[0].dtype)


def _ep_silu(acc, extras, outs):
    outs[0][...] = _silu(acc).astype(outs[0].dtype)


def _ep_hgrn_q(acc, extras, outs):
    outs[0][...] = (_silu(acc) * (HEAD_DIM ** -0.5)).astype(outs[0].dtype)


def _ep_hgrn_logf(acc, extras, outs):
    lb = extras[0]
    outs[0][...] = jnp.log(lb + (1.0 - lb) * jax.nn.sigmoid(acc))


_DSA_S_LAT = 0
_DSA_S_QI = DSA_LATENT
_DSA_S_KE = _DSA_S_QI + IDX_HEADS * IDX_DIM
_DSA_S_KO = _DSA_S_KE + 2 * IDX_DIM
_DSA_S_WI = _DSA_S_KO + 2 * IDX_DIM
_DSA_S_N = _DSA_S_WI + 128


def _ep_dsa_small(acc, extras, outs):
    lat_ref, qi_ref, ke_ref, ko_ref, wi_ref = outs
    _rms_heads(acc[:, :DSA_LATENT], extras[0][:, :DSA_LATENT], DSA_LATENT, 1.0, lat_ref)
    qi_ref[...] = acc[:, _DSA_S_QI:_DSA_S_KE].astype(BF16)
    ke_ref[...] = acc[:, _DSA_S_KE:_DSA_S_KO].astype(BF16)
    ko_ref[...] = acc[:, _DSA_S_KO:_DSA_S_WI].astype(BF16)
    wi_ref[...] = acc[:, _DSA_S_WI:_DSA_S_WI + IDX_HEADS] * (IDX_HEADS ** -0.5 * IDX_DIM ** -0.5)


def _out_kernel(a_ref, w_ref, x_ref, g_ref, o_ref):
    o_ref[...] = x_ref[...] + g_ref[0] * _dot(a_ref[...], w_ref[...])


def _out_proj(a, w, x2, gate, seq, tm, tn=1024):
    t, k = a.shape
    d = w.shape[1]
    bsz = gate.shape[0]
    per_seq = seq // tm
    return pl.pallas_call(
        _out_kernel,
        grid=(d // tn, t // tm),
        in_specs=[
            pl.BlockSpec((tm, k), lambda j, m: (m, 0)),
            pl.BlockSpec((k, tn), lambda j, m: (0, j)),
            pl.BlockSpec((tm, tn), lambda j, m: (m, j)),
            pl.BlockSpec((1, 1, tn), lambda j, m: (m // per_seq, 0, j)),
        ],
        out_specs=pl.BlockSpec((tm, tn), lambda j, m: (m, j)),
        out_shape=jax.ShapeDtypeStruct((t, d), F32),
        compiler_params=_params("parallel", "parallel"),
        name="out_proj",
    )(a, w, x2, gate.reshape(bsz, 1, d))


def _ffn_in_kernel(h_ref, wu_ref, wg_ref, cw_ref, cb_ref, a_ref, carry_ref, wub_ref, wgb_ref,
                   *, per_seq, rc, cc):
    m = pl.program_id(1)
    tm, tn = a_ref.shape

    @pl.when(m == 0)
    def _():
        wub_ref[...] = wu_ref[0].astype(BF16)
        wgb_ref[...] = wg_ref[0].astype(BF16)

    @pl.when(m % per_seq == 0)
    def _():
        carry_ref[...] = jnp.zeros_like(carry_ref)

    cw = cw_ref[0]
    cb = cb_ref[0]
    row = lax.broadcasted_iota(jnp.int32, (rc, cc), 0)
    for c in range(tn // cc):
        cols = slice(c * cc, (c + 1) * cc)
        prev = carry_ref[:, cols]
        for r in range(tm // rc):
            rows = slice(r * rc, (r + 1) * rc)
            h = h_ref[rows, :]
            u = _dot(h, wub_ref[:, cols])
            g = _dot(h, wgb_ref[:, cols])
            p1 = prev[7:8, :]
            p2 = prev[6:7, :]
            g1 = jnp.where(row == 0, p1, pltpu.roll(g, 1, axis=0))
            g2 = jnp.where(row == 0, p2, jnp.where(row == 1, p1, pltpu.roll(g, 2, axis=0)))
            conv = cw[0:1, cols] * g2 + cw[1:2, cols] * g1 + cw[2:3, cols] * g + cb[:, cols]
            a_ref[rows, cols] = (_silu(conv) * u).astype(BF16)
            prev = g[rc - 8:, :]
        carry_ref[:, cols] = prev


def _ffn_in(h, w_all, conv_w, conv_b, layer, seq, tm=2048, tn=512, rc=512, cc=256):
    t, k = h.shape
    f = w_all.shape[2] // 2
    nf = f // tn
    tm = min(tm, seq)
    per_seq = seq // tm
    rc = min(rc, tm)
    return pl.pallas_call(
        functools.partial(_ffn_in_kernel, per_seq=per_seq, rc=rc, cc=cc),
        grid=(nf, t // tm),
        in_specs=[
            pl.BlockSpec((tm, k), lambda j, m: (m, 0)),
            pl.BlockSpec((1, k, tn), lambda j, m: (layer, 0, j)),
            pl.BlockSpec((1, k, tn), lambda j, m: (layer, 0, j + nf)),
            pl.BlockSpec((1, CONV_WIDTH, tn), lambda j, m: (layer, 0, j)),
            pl.BlockSpec((1, 1, tn), lambda j, m: (layer, 0, j)),
        ],
        out_specs=pl.BlockSpec((tm, tn), lambda j, m: (m, j)),
        out_shape=jax.ShapeDtypeStruct((t, f), BF16),
        scratch_shapes=[
            pltpu.VMEM((8, tn), F32),
            pltpu.VMEM((k, tn), BF16),
            pltpu.VMEM((k, tn), BF16),
        ],
        compiler_params=_params("parallel", "arbitrary"),
        name="ffn_in",
    )(h, w_all, w_all, conv_w, conv_b)


def _dsa_select_kernel(qi_ref, ke_ref, ko_ref, wit_ref, bias_ref, key_ref, cut_ref, *, topk, ts):
    i = pl.program_id(1)
    tq = qi_ref.shape[0]
    seq = ke_ref.shape[0]
    n_tiles = ((i + 1) * tq + ts - 1) // ts
    qi = qi_ref[...]
    qit = jnp.concatenate([qi[:, 128 * j:128 * (j + 1)].astype(F32).T.astype(BF16)
                           for j in range(IDX_HEADS // 2)], axis=1)
    wit = wit_ref[...]
    tpos = i * tq + lax.broadcasted_iota(jnp.int32, (ts, tq), 1)
    first_hidden = (tpos // CHUNK + 1) * CHUNK
    row = lax.broadcasted_iota(jnp.int32, (ts, tq), 0)

    def score_tile(t, carry):
        k0 = pl.multiple_of(t * ts, ts)
        sce = _dot(ke_ref[pl.ds(k0, ts), :], qit)
        sco = _dot(ko_ref[pl.ds(k0, ts), :], qit)
        score = jnp.zeros((ts, tq), F32)
        for j in range(IDX_HEADS // 2):
            cols = slice(j * tq, (j + 1) * tq)
            score = score + wit[2 * j:2 * j + 1] * jnp.maximum(sce[:, cols], 0.0)
            score = score + wit[2 * j + 1:2 * j + 2] * jnp.maximum(sco[:, cols], 0.0)
        score = score + 0.0
        bits = pltpu.bitcast(score, jnp.int32)
        key = jnp.where(bits < 0, bits ^ jnp.int32(0x7FFFFFFF), bits)
        key_ref[pl.ds(k0, ts), :] = jnp.where(k0 + row < first_hidden, key, INT_MIN)
        return carry

    lax.fori_loop(0, n_tiles, score_tile, 0)
    kf = jnp.float32(topk)

    def count(hit_fn):
        def body(t, cnt8):
            k0 = pl.multiple_of(t * ts, ts)
            hit = hit_fn(key_ref[pl.ds(k0, ts), :], k0)
            return cnt8 + jnp.sum(hit.reshape(ts // 8, 8, tq), axis=0)
        cnt8 = lax.fori_loop(0, n_tiles, body, jnp.zeros((8, tq), F32))
        return jnp.sum(cnt8, axis=0, keepdims=True)

    def count_ge(cand):
        return count(lambda key, k0: jnp.where(key >= cand, 1.0, 0.0))

    visible = count_ge(jnp.full((1, tq), INT_MIN + 1, jnp.int32))
    c_pos = count_ge(jnp.zeros((1, tq), jnp.int32))
    thr0 = jnp.where(c_pos >= kf, 0, INT_MIN).astype(jnp.int32)
    cnt0 = jnp.where(c_pos >= kf, c_pos, jnp.float32(seq + 1))

    def bit_step(b, state):
        thr, cnt = state
        cand = thr + lax.shift_left(jnp.int32(1), jnp.int32(30) - b)
        c = count_ge(cand)
        take = c >= kf
        return jnp.where(take, cand, thr), jnp.where(take, c, cnt)

    thr, cnt = lax.fori_loop(0, 31, bit_step, (thr0, cnt0))

    cut_ref[...] = jnp.full((1, tq), seq, jnp.int32)
    tied = jnp.where(cnt > kf, jnp.where(visible > kf, 1.0, 0.0), 0.0)

    @pl.when(jnp.max(tied) > 0.0)
    def _():
        need = kf - count(lambda key, k0: jnp.where(key > thr, 1.0, 0.0))
        nbits = seq.bit_length()

        def idx_step(b, cut):
            cand = cut + lax.shift_left(jnp.int32(1), jnp.int32(nbits - 1) - b)
            c = count(lambda key, k0: jnp.where(key == thr, jnp.where(k0 + row < cand, 1.0, 0.0), 0.0))
            return jnp.where(c <= need, cand, cut)

        cut = lax.fori_loop(0, nbits, idx_step, jnp.zeros((1, tq), jnp.int32))
        cut_ref[...] = jnp.where(tied > 0.0, cut, seq)

    def write_tile(t, carry):
        k0 = pl.multiple_of(t * ts, ts)
        key = key_ref[pl.ds(k0, ts), :]
        tie_keep = jnp.where(k0 + row < cut_ref[...], 0.0, NEG_BIG)
        keep = jnp.where(key > thr, 0.0,
                         jnp.where(key == thr, jnp.where(key > INT_MIN, tie_keep, NEG_BIG), NEG_BIG))
        bias_ref[0, pl.ds(k0, ts), :] = keep.astype(BF16)
        return carry

    def fill_tile(t, carry):
        k0 = pl.multiple_of(t * ts, ts)
        bias_ref[0, pl.ds(k0, ts), :] = jnp.full((ts, tq), NEG_BIG, BF16)
        return carry

    lax.fori_loop(0, n_tiles, write_tile, 0)
    lax.fori_loop(n_tiles, seq // ts, fill_tile, 0)


def _dsa_select(qi, ke, ko, wit, bsz, seq, tq=256, ts=256):
    nq = seq // tq
    topk = min(TOPK_MAX, seq // 4)
    return pl.pallas_call(
        functools.partial(_dsa_select_kernel, topk=topk, ts=ts),
        grid=(bsz, nq),
        in_specs=[
            pl.BlockSpec((tq, IDX_HEADS * IDX_DIM), lambda b, i: (b * nq + i, 0)),
            pl.BlockSpec((seq, 2 * IDX_DIM), lambda b, i: (b, 0)),
            pl.BlockSpec((seq, 2 * IDX_DIM), lambda b, i: (b, 0)),
            pl.BlockSpec((IDX_HEADS, tq), lambda b, i: (0, b * nq + i)),
        ],
        out_specs=pl.BlockSpec((1, seq, tq), lambda b, i: (b, 0, i)),
        out_shape=jax.ShapeDtypeStruct((bsz, seq, seq), BF16),
        scratch_shapes=[pltpu.VMEM((seq, tq), jnp.int32), pltpu.VMEM((1, tq), jnp.int32)],
        compiler_params=_params("parallel", "parallel"),
        name="dsa_select",
    )(qi, ke, ko, wit)


def _dsa_attn_kernel(q_ref, lat_ref, latt_ref, bias_ref, o_ref, m_ref, acc_ref,
                     s0_ref, s1_ref, p0_ref, p1_ref, a0_ref, a1_ref, *, hg, ng, tk):
    i = pl.program_id(1)
    tq = q_ref.shape[0]
    r = DSA_LATENT
    last_tile = lat_ref.shape[0] // tk - 1
    n_pairs = ((i + 1) * tq + 2 * tk - 1) // (2 * tk)
    tpos = i * tq + lax.broadcasted_iota(jnp.int32, (tk, tq), 1)
    row = lax.broadcasted_iota(jnp.int32, (tk, tq), 0)
    for g0 in range(0, DSA_HEADS // hg, ng):
        groups = [range((g0 + n) * hg, (g0 + n + 1) * hg) for n in range(ng)]
        q4t = [jnp.concatenate([q_ref[:, h * r:(h + 1) * r].astype(F32).T.astype(BF16) for h in heads], axis=1)
               for heads in groups]

        def scores(j, s_ref):
            k0 = pl.multiple_of(j * tk, tk)
            lat = lat_ref[pl.ds(k0, tk), :]
            for n in range(ng):
                s_ref[n] = _dot(lat, q4t[n])

        def softmax(j, s_ref, p_ref, a_ref):
            k0 = pl.multiple_of(j * tk, tk)
            dist = jnp.abs(tpos - (k0 + row)).astype(F32)
            bias = bias_ref[0, pl.ds(k0, tk), :].astype(F32)
            for n, heads in enumerate(groups):
                for idx, h in enumerate(heads):
                    cs = slice(idx * tq, (idx + 1) * tq)
                    slope = LOG2E * 2.0 ** (-8.0 * (h + 1) / DSA_HEADS)
                    s = s_ref[n, :, cs] + (bias - slope * dist)
                    m_old = m_ref[n, :, cs]
                    m_new = jnp.maximum(m_old, jnp.max(s, axis=0, keepdims=True))
                    m_ref[n, :, cs] = m_new
                    p_ref[n, :, cs] = jnp.exp2(s - m_new).astype(BF16)
                    a_ref[n, :, cs] = jnp.exp2(m_old - m_new)

        def values(j, p_ref, a_ref):
            k0 = pl.multiple_of(j * tk, tk)
            latt = jnp.concatenate([latt_ref[:, pl.ds(k0, tk)], jnp.ones((ONES_ROWS, tk), BF16)], axis=0)
            pvs = [_dot(latt, p_ref[n]) for n in range(ng)]
            for n in range(ng):
                acc_ref[n] = a_ref[n] * acc_ref[n] + pvs[n]

        m_ref[...] = jnp.full(m_ref.shape, NEG_BIG, F32)
        acc_ref[...] = jnp.zeros(acc_ref.shape, F32)
        scores(0, s0_ref)

        def trip(ta, first):
            scores(ta + 1, s1_ref)
            if not first:
                values(ta - 1, p1_ref, a1_ref)
            softmax(ta, s0_ref, p0_ref, a0_ref)
            scores(jnp.minimum(ta + 2, last_tile), s0_ref)
            values(ta, p0_ref, a0_ref)
            softmax(ta + 1, s1_ref, p1_ref, a1_ref)

        def body(jj, carry):
            trip(2 * jj, False)
            return carry

        trip(0, True)
        lax.fori_loop(1, n_pairs, body, 0)
        values(2 * n_pairs - 1, p1_ref, a1_ref)
        for n, heads in enumerate(groups):
            out = (acc_ref[n, :r, :] / acc_ref[n, r:r + 1, :]).T
            for idx, h in enumerate(heads):
                o_ref[:, h * r:(h + 1) * r] = out[idx * tq:(idx + 1) * tq].astype(BF16)


def _dsa_attn(q, lat, latt, bias, bsz, seq, tq=128, hg=4, ng=4, tk=256):
    nq = seq // tq
    width = DSA_HEADS * DSA_LATENT
    cols = hg * tq
    assert (seq // tk) % 2 == 0
    return pl.pallas_call(
        functools.partial(_dsa_attn_kernel, hg=hg, ng=ng, tk=tk),
        grid=(bsz, nq),
        in_specs=[
            pl.BlockSpec((tq, width), lambda b, i: (b * nq + i, 0)),
            pl.BlockSpec((seq, DSA_LATENT), lambda b, i: (b, 0)),
            pl.BlockSpec((DSA_LATENT, seq), lambda b, i: (0, b)),
            pl.BlockSpec((1, seq, tq), lambda b, i: (b, 0, i)),
        ],
        out_specs=pl.BlockSpec((tq, width), lambda b, i: (b * nq + i, 0)),
        out_shape=jax.ShapeDtypeStruct((bsz * seq, width), BF16),
        scratch_shapes=[
            pltpu.VMEM((ng, 1, cols), F32),
            pltpu.VMEM((ng, DSA_LATENT + ONES_ROWS, cols), F32),
            pltpu.VMEM((ng, tk, cols), F32),
            pltpu.VMEM((ng, tk, cols), F32),
            pltpu.VMEM((ng, tk, cols), BF16),
            pltpu.VMEM((ng, tk, cols), BF16),
            pltpu.VMEM((ng, 1, cols), F32),
            pltpu.VMEM((ng, 1, cols), F32),
        ],
        compiler_params=_params("parallel", "parallel"),
        name="dsa_attn",
    )(q, lat, latt, bias)


def _split3(x):
    hi = x.astype(BF16)
    r1 = x - hi.astype(F32)
    mid = r1.astype(BF16)
    lo = (r1 - mid.astype(F32)).astype(BF16)
    return hi, mid, lo


def _fox_cum_kernel(h_ref, wt_ref, b_ref, cum_ref, *, blk):
    fl = _dot_nt(wt_ref[...], h_ref[...]) + b_ref[...]
    logf = jax.nn.log_sigmoid(fl)
    seq = logf.shape[1]
    r = lax.broadcasted_iota(jnp.int32, (blk, blk), 0)
    c = lax.broadcasted_iota(jnp.int32, (blk, blk), 1)
    upper = jnp.where(r <= c, 1.0, 0.0).astype(BF16)
    carry = jnp.zeros((logf.shape[0], 1), F32)
    for j in range(seq // blk):
        hi, mid, lo = _split3(logf[:, j * blk:(j + 1) * blk])
        cs = (_dot(hi, upper) + _dot(mid, upper)) + _dot(lo, upper) + carry
        cum_ref[0, :, j * blk:(j + 1) * blk] = cs
        carry = cs[:, blk - 1:blk]


def _fox_cum(h, w_fl_t, b_f, bsz, seq, blk=256):
    d = h.shape[1]
    return pl.pallas_call(
        functools.partial(_fox_cum_kernel, blk=blk),
        grid=(bsz,),
        in_specs=[
            pl.BlockSpec((seq, d), lambda b: (b, 0)),
            pl.BlockSpec((FOX_HEADS, d), lambda b: (0, 0)),
            pl.BlockSpec((FOX_HEADS, 1), lambda b: (0, 0)),
        ],
        out_specs=pl.BlockSpec((1, FOX_HEADS, seq), lambda b: (b, 0, 0)),
        out_shape=jax.ShapeDtypeStruct((bsz, FOX_HEADS, seq), F32),
        compiler_params=_params("parallel"),
        name="fox_cum",
    )(h, w_fl_t, b_f.reshape(FOX_HEADS, 1))


def _fox_attn_kernel(q_ref, k_ref, vt_ref, g_ref, o_ref, m_ref, acc_ref):
    i = pl.program_id(2)
    tq = q_ref.shape[0]
    hb = g_ref.shape[1] // HEAD_DIM
    qts = [q_ref[:, h * FOX_AUG:(h + 1) * FOX_AUG].astype(F32).T.astype(BF16) for h in range(hb)]
    m_ref[...] = jnp.full(m_ref.shape, NEG_BIG, F32)
    acc_ref[...] = jnp.zeros(acc_ref.shape, F32)
    ones = jnp.ones((ONES_ROWS, tq), BF16)

    def tile(j, diagonal):
        k0 = pl.multiple_of(j * tq, tq)
        ss = [_dot(k_ref[pl.ds(k0, tq), h * FOX_AUG:(h + 1) * FOX_AUG], qts[h])
              for h in range(hb)]
        ps, alphas = [], []
        for h in range(hb):
            s = ss[h]
            if diagonal:
                key = lax.broadcasted_iota(jnp.int32, (tq, tq), 0)
                qry = lax.broadcasted_iota(jnp.int32, (tq, tq), 1)
                s = jnp.where(key <= qry, s, NEG_BIG)
            m_old = m_ref[h]
            m_new = jnp.maximum(m_old, jnp.max(s, axis=0, keepdims=True))
            m_ref[h] = m_new
            ps.append(jnp.exp2(s - m_new).astype(BF16))
            alphas.append(jnp.exp2(m_old - m_new))
        pvs = [_dot(jnp.concatenate([vt_ref[h * HEAD_DIM:(h + 1) * HEAD_DIM, pl.ds(k0, tq)], ones], axis=0),
                    ps[h]) for h in range(hb)]
        for h in range(hb):
            acc_ref[h] = alphas[h] * acc_ref[h] + pvs[h]

    def body(j, carry):
        tile(j, False)
        return carry

    lax.fori_loop(0, i, body, 0)
    tile(i, True)
    for h in range(hb):
        cols = slice(h * HEAD_DIM, (h + 1) * HEAD_DIM)
        o = (acc_ref[h, :HEAD_DIM, :] / acc_ref[h, HEAD_DIM:HEAD_DIM + 1, :]).T
        o_ref[:, cols] = (o * g_ref[:, cols].astype(F32)).astype(BF16)


def _fox_attn(q, k, vt, gate, bsz, seq, tq=256, hb=8):
    nq = seq // tq
    d = gate.shape[1]
    qmap = lambda b, h, i: (b * nq + i, h)
    return pl.pallas_call(
        _fox_attn_kernel,
        grid=(bsz, FOX_HEADS // hb, nq),
        in_specs=[
            pl.BlockSpec((tq, hb * FOX_AUG), qmap),
            pl.BlockSpec((seq, hb * FOX_AUG), lambda b, h, i: (b, h)),
            pl.BlockSpec((hb * HEAD_DIM, seq), lambda b, h, i: (h, b)),
            pl.BlockSpec((tq, hb * HEAD_DIM), qmap),
        ],
        out_specs=pl.BlockSpec((tq, hb * HEAD_DIM), qmap),
        out_shape=jax.ShapeDtypeStruct((bsz * seq, d), BF16),
        scratch_shapes=[
            pltpu.VMEM((hb, 1, tq), F32),
            pltpu.VMEM((hb, HEAD_DIM + ONES_ROWS, tq), F32),
        ],
        compiler_params=_params("parallel", "parallel", "parallel"),
        name="fox_attn",
    )(q, k, vt, gate)


HGRN_SUB = 8


def _hgrn_kernel(q_ref, lg_ref, v_ref, gate_ref, gain_ref, o_ref,
                 state_ref, oacc_ref, kpad_ref, fpad_ref, vpad_ref, *, hb):
    c = pl.program_id(2)
    cl = q_ref.shape[0]
    dk = HEAD_DIM
    sub = HGRN_SUB

    @pl.when(c == 0)
    def _():
        state_ref[...] = jnp.zeros_like(state_ref)
        kpad_ref[0:sub, :] = jnp.zeros((sub, dk), F32)
        fpad_ref[0:sub, :] = jnp.zeros((sub, dk), F32)
        vpad_ref[0:sub, :] = jnp.zeros((sub, dk), F32)

    r = lax.broadcasted_iota(jnp.int32, (cl, cl), 0)
    cc = lax.broadcasted_iota(jnp.int32, (cl, cl), 1)
    lower = jnp.where(cc <= r, 1.0, 0.0).astype(BF16)
    block_start = lax.broadcasted_iota(jnp.int32, (cl, dk), 0) % sub == 0

    for hh in range(hb):
        cols = slice(hh * dk, (hh + 1) * dk)
        q = q_ref[:, cols].astype(F32)
        lg = lg_ref[:, cols]
        vb = v_ref[:, cols]
        v = vb.astype(F32)
        hi, mid, lo = _split3(lg)
        gcum = (_dot(lower, hi) + _dot(lower, mid)) + _dot(lower, lo)
        f = jnp.exp(lg)
        k = 1.0 - f
        st = state_ref[hh]

        oacc_ref[...] = _dot_nt((q * jnp.exp(gcum)).astype(BF16), st.astype(BF16))

        m = cl // 2
        while m >= sub:
            for blk in range(cl // (2 * m)):
                r0 = blk * 2 * m
                mid_row = r0 + m
                gm = gcum[mid_row - 1:mid_row, :]
                qs = (q[mid_row:mid_row + m] * jnp.exp(gcum[mid_row:mid_row + m] - gm)).astype(BF16)
                ks = (k[r0:mid_row] * jnp.exp(gm - gcum[r0:mid_row])).astype(BF16)
                sc = _dot_nt(qs, ks)
                oacc_ref[mid_row:mid_row + m, :] += _dot(sc.astype(BF16), vb[r0:mid_row])
            m //= 2

        kpad_ref[sub:, :] = k
        fpad_ref[sub:, :] = jnp.where(block_start, 0.0, f)
        vpad_ref[sub:, :] = v
        diag = jnp.sum(q * k, axis=-1, keepdims=True) * v
        dec = None
        for delta in range(1, sub):
            fd = fpad_ref[sub - delta + 1:sub - delta + 1 + cl, :]
            dec = fd if dec is None else dec * fd
            kd = kpad_ref[sub - delta:sub - delta + cl, :]
            vd = vpad_ref[sub - delta:sub - delta + cl, :]
            diag = diag + jnp.sum(q * kd * dec, axis=-1, keepdims=True) * vd
        o = oacc_ref[...] + diag

        gl = gcum[cl - 1:cl, :]
        kdec = (k * jnp.exp(gl - gcum)).astype(BF16)
        state_ref[hh] = st * jnp.exp(gl) + _dot_tn(vb, kdec)

        ms = jnp.mean(o * o, axis=-1, keepdims=True)
        y = o * lax.rsqrt(ms + EPS) * gain_ref[...]
        o_ref[:, cols] = (y * gate_ref[:, cols].astype(F32)).astype(BF16)


def _hgrn_core(q, lg, v, gate, gain, bsz, seq, cl=256, hb=8):
    nc = seq // cl
    d = q.shape[1]
    blk = lambda b, g, c: (b * nc + c, g)
    spec = pl.BlockSpec((cl, hb * HEAD_DIM), blk)
    return pl.pallas_call(
        functools.partial(_hgrn_kernel, hb=hb),
        grid=(bsz, HGRN_HEADS // hb, nc),
        in_specs=[spec, spec, spec, spec, pl.BlockSpec((1, HEAD_DIM), lambda b, g, c: (0, 0))],
        out_specs=spec,
        out_shape=jax.ShapeDtypeStruct((bsz * seq, d), BF16),
        scratch_shapes=[
            pltpu.VMEM((hb, HEAD_DIM, HEAD_DIM), F32),
            pltpu.VMEM((cl, HEAD_DIM), F32),
            pltpu.VMEM((cl + HGRN_SUB, HEAD_DIM), F32),
            pltpu.VMEM((cl + HGRN_SUB, HEAD_DIM), F32),
            pltpu.VMEM((cl + HGRN_SUB, HEAD_DIM), F32),
        ],
        compiler_params=_params("parallel", "parallel", "arbitrary"),
        name="hgrn_core",
    )(q, lg, v, gate, gain.reshape(1, HEAD_DIM))


def _dsa_mixer(h, w_in, q_gain, kv_gain, bsz, seq):
    d = h.shape[1]
    nq = DSA_HEADS * DSA_LATENT
    lat_w = w_in[:, nq:nq + DSA_LATENT]
    qi_w = w_in[:, nq + DSA_LATENT:nq + DSA_LATENT + IDX_HEADS * IDX_DIM]
    c0 = nq + DSA_LATENT + IDX_HEADS * IDX_DIM
    ki_w = w_in[:, c0:c0 + IDX_DIM]
    wi_w = w_in[:, c0 + IDX_DIM:c0 + IDX_DIM + IDX_HEADS]
    z = jnp.zeros((d, IDX_DIM), F32)
    w_s = jnp.concatenate(
        [lat_w, qi_w, ki_w, z, z, ki_w, wi_w, jnp.zeros((d, 128 - IDX_HEADS), F32)], axis=1).astype(BF16)
    q_gain_row = jnp.tile(q_gain, DSA_HEADS).reshape(1, nq)
    kv_gain_row = jnp.zeros((1, _DSA_S_N), F32).at[0, :DSA_LATENT].set(kv_gain)

    (q,) = _proj(h, w_in, 0, nq, _ep_dsa_q, [(1024, nq, BF16)], extras=(q_gain_row,))
    lat, qi, ke, ko, wi = _proj(
        h, w_s, 0, _DSA_S_N, _ep_dsa_small,
        [(DSA_LATENT, DSA_LATENT, BF16), (IDX_HEADS * IDX_DIM, IDX_HEADS * IDX_DIM, BF16),
         (2 * IDX_DIM, 2 * IDX_DIM, BF16), (2 * IDX_DIM, 2 * IDX_DIM, BF16), (IDX_HEADS, IDX_HEADS, F32)],
        extras=(kv_gain_row,), tn=_DSA_S_N)
    bias = _dsa_select(qi, ke, ko, wi.T, bsz, seq)
    return _dsa_attn(q, lat, lat.T, bias, bsz, seq)


def _fox_mixer(h, w_in, b_f, q_gain, k_gain, bsz, seq):
    d = h.shape[1]
    w = w_in
    w_fl_t = w_in[:, 4 * d:].T.astype(BF16)
    qg = jnp.tile(q_gain, FOX_HEADS).reshape(1, d)
    kg = jnp.tile(k_gain, FOX_HEADS).reshape(1, d)
    cum = _fox_cum(h, w_fl_t, b_f, bsz, seq)
    cum_t = cum.transpose(0, 2, 1).reshape(bsz * seq, FOX_HEADS)
    aug = [(1024 // HEAD_DIM * FOX_AUG, FOX_HEADS * FOX_AUG, BF16)]
    (q,) = _proj(h, w, 0, d, _ep_fox_q, aug, extras=(qg,))
    (k,) = _proj(h, w, d, d, _ep_fox_k, aug, extras=(kg,), row_extras=(cum_t,))
    vt = _proj_t(h, w_in[:, 2 * d:3 * d].T.astype(BF16))
    (gate,) = _proj(h, w, 3 * d, d, _ep_sigmoid, [(1024, d, BF16)])
    return _fox_attn(q, k, vt, gate, bsz, seq)


def _hgrn_mixer(h, w_in, lb, o_gain, bsz, seq):
    d = h.shape[1]
    w = w_in
    (q,) = _proj(h, w, 0, d, _ep_hgrn_q, [(1024, d, BF16)])
    (lg,) = _proj(h, w, d, d, _ep_hgrn_logf, [(1024, d, F32)], extras=(lb.reshape(1, d),))
    (v,) = _proj(h, w, 2 * d, d, _ep_cast, [(1024, d, BF16)])
    (gate,) = _proj(h, w, 3 * d, d, _ep_silu, [(1024, d, BF16)])
    return _hgrn_core(q, lg, v, gate, o_gain, bsz, seq)


def kernel(x, c, ada_w, ada_b, norm_mix_g, norm_ffn_g, dsa_w_in, dsa_q_norm, dsa_kv_norm, dsa_w_out, fox_w_in, fox_b_f, fox_q_norm, fox_k_norm, fox_w_out, hgrn_w_in, hgrn_lb, hgrn_o_norm, hgrn_w_out, ffn_w_in, ffn_conv_w, ffn_conv_b, ffn_w_out):
    bsz, seq, d = x.shape
    depth = ada_w.shape[0]
    f = ffn_conv_b.shape[1]

    mod = _ada_all(c, ada_w, ada_b)
    lb_soft = jax.nn.softmax(hgrn_lb.astype(F32), axis=0)
    lb_all = jnp.cumsum(lb_soft, axis=0) - lb_soft[0]
    conv_b = ffn_conv_b.reshape(depth, 1, f)

    x2 = x.reshape(bsz * seq, d)
    for i in range(depth):
        sh1, sc1, g1, sh2, sc2, g2 = [mod[i, :, j * d:(j + 1) * d] for j in range(6)]
        h = _norm_mod(x2, norm_mix_g[i], sh1, sc1, seq)
        kind, j = i % 3, i // 3
        if kind == 0:
            a = _dsa_mixer(h, dsa_w_in[j], dsa_q_norm[j], dsa_kv_norm[j], bsz, seq)
            w_out = dsa_w_out[j]
        elif kind == 1:
            a = _fox_mixer(h, fox_w_in[j], fox_b_f[j], fox_q_norm[j], fox_k_norm[j], bsz, seq)
            w_out = fox_w_out[j]
        else:
            a = _hgrn_mixer(h, hgrn_w_in[j], lb_all[i], hgrn_o_norm[j], bsz, seq)
            w_out = hgrn_w_out[j]
        x2 = _out_proj(a, w_out.astype(BF16), x2, g1, seq, tm=1024)
        h = _norm_mod(x2, norm_ffn_g[i], sh2, sc2, seq)
        a = _ffn_in(h, ffn_w_in, ffn_conv_w, conv_b, i, seq)
        x2 = _out_proj(a, ffn_w_out[i].astype(BF16), x2, g2, seq, tm=512)
    return x2.reshape(bsz, seq, d)
```

```python
import functools

import jax
import jax.numpy as jnp
from jax import lax
from jax.experimental import pallas as pl
from jax.experimental.pallas import tpu as pltpu

F32 = jnp.float32
BF16 = jnp.bfloat16

EPS = 1e-6
CHUNK = 64
TOPK_MAX = 256

DSA_HEADS = 16
DSA_LATENT = 256
IDX_HEADS = 16
IDX_DIM = 64
FOX_HEADS = 16
HGRN_HEADS = 16
HEAD_DIM = 128
CONV_WIDTH = 3

NEG_BIG = -1e30
LOG2E = 1.4426950408889634
ONES_ROWS = 16
INT_MIN = -2 ** 31

V7X_VMEM_BYTES = 64 * 1024 * 1024
VMEM_LIMIT = 56 * 1024 * 1024


def _params(*sem):
    return pltpu.CompilerParams(dimension_semantics=sem, vmem_limit_bytes=VMEM_LIMIT)


def _dot(a, b):
    return jnp.dot(a, b, preferred_element_type=F32)


def _dot_nt(a, b):
    return lax.dot_general(a, b, (((1,), (1,)), ((), ())), preferred_element_type=F32)


def _dot_tn(a, b):
    return lax.dot_general(a, b, (((0,), (0,)), ((), ())), preferred_element_type=F32)


def _silu(x):
    return x * jax.nn.sigmoid(x)


def _ada_kernel(c_ref, w_ref, b_ref, o_ref):
    cond = _silu(c_ref[...]).astype(BF16)
    o_ref[0] = _dot(cond, w_ref[0].astype(BF16)) + b_ref[0]


def _ada_all(c, ada_w, ada_b, tn=1024):
    depth, d, n = ada_w.shape
    bsz = c.shape[0]
    rows = 8
    c_pad = jnp.zeros((rows, d), F32).at[:bsz].set(c)
    out = pl.pallas_call(
        _ada_kernel,
        grid=(depth, n // tn),
        in_specs=[
            pl.BlockSpec((rows, d), lambda l, j: (0, 0)),
            pl.BlockSpec((1, d, tn), lambda l, j: (l, 0, j)),
            pl.BlockSpec((1, 1, tn), lambda l, j: (l, 0, j)),
        ],
        out_specs=pl.BlockSpec((1, rows, tn), lambda l, j: (l, 0, j)),
        out_shape=jax.ShapeDtypeStruct((depth, rows, n), F32),
        compiler_params=_params("parallel", "parallel"),
        name="ada_mod",
    )(c_pad, ada_w, ada_b.reshape(depth, 1, n))
    return out[:, :bsz]


def _norm_mod_kernel(x_ref, g_ref, sh_ref, sc_ref, h_ref):
    x = x_ref[...]
    ms = jnp.mean(x * x, axis=-1, keepdims=True)
    y = x * lax.rsqrt(ms + EPS) * g_ref[...]
    h_ref[...] = (y * (1.0 + sc_ref[0]) + sh_ref[0]).astype(BF16)


def _norm_mod(x2, g, shift, scale, seq, tm=512):
    t, d = x2.shape
    per_seq = seq // tm
    bsz = shift.shape[0]
    vec = lambda m: (m // per_seq, 0, 0)
    return pl.pallas_call(
        _norm_mod_kernel,
        grid=(t // tm,),
        in_specs=[
            pl.BlockSpec((tm, d), lambda m: (m, 0)),
            pl.BlockSpec((1, d), lambda m: (0, 0)),
            pl.BlockSpec((1, 1, d), vec),
            pl.BlockSpec((1, 1, d), vec),
        ],
        out_specs=pl.BlockSpec((tm, d), lambda m: (m, 0)),
        out_shape=jax.ShapeDtypeStruct((t, d), BF16),
        compiler_params=_params("parallel"),
        name="norm_mod",
    )(x2, g.reshape(1, d), shift.reshape(bsz, 1, d), scale.reshape(bsz, 1, d))


def _proj_kernel(*refs, epilogue, n_extra, cast_w, w_t):
    h_ref, w_ref = refs[0], refs[1]
    extras = refs[2:2 + n_extra]
    if cast_w:
        outs, wb_ref = refs[2 + n_extra:-1], refs[-1]

        @pl.when(pl.program_id(1) == 0)
        def _():
            w32 = w_ref[0].T if w_t else w_ref[...]
            wb_ref[...] = w32.astype(BF16)

        w = wb_ref[...]
    else:
        outs = refs[2 + n_extra:]
        w = w_ref[...]
    acc = _dot(h_ref[...], w)
    epilogue(acc, [e[...] for e in extras], outs)


def _proj(h, w, col0, n, epilogue, out_defs, extras=(), row_extras=(), tm=1024, tn=1024, layer=None):
    t, k = h.shape
    tn = min(tn, n)
    tm = min(tm, t)
    assert n % tn == 0 and col0 % tn == 0 and t % tm == 0
    off = col0 // tn
    in_specs = [
        pl.BlockSpec((tm, k), lambda j, m: (m, 0)),
        pl.BlockSpec((k, tn), lambda j, m: (0, j + off)) if layer is None else
        pl.BlockSpec((1, tn, k), lambda j, m: (layer, j + off, 0)),
    ] + [pl.BlockSpec((1, tn), lambda j, m: (0, j)) for _ in extras
         ] + [pl.BlockSpec((tm, r.shape[1]), lambda j, m: (m, 0)) for r in row_extras]
    extras = tuple(extras) + tuple(row_extras)
    out_specs = [pl.BlockSpec((tm, wt), lambda j, m: (m, j)) for wt, _, _ in out_defs]
    out_shape = [jax.ShapeDtypeStruct((t, wtot), dt) for _, wtot, dt in out_defs]
    cast_w = w.dtype != BF16
    return pl.pallas_call(
        functools.partial(_proj_kernel, epilogue=epilogue, n_extra=len(extras), cast_w=cast_w,
                          w_t=layer is not None),
        grid=(n // tn, t // tm),
        in_specs=in_specs,
        out_specs=out_specs,
        out_shape=out_shape,
        scratch_shapes=[pltpu.VMEM((k, tn), BF16)] if cast_w else [],
        compiler_params=_params("parallel", "arbitrary"),
        name="proj_" + epilogue.__name__.strip("_"),
    )(h, w, *extras)


def _proj_t_kernel(wt_ref, h_ref, o_ref):
    o_ref[...] = _dot_nt(wt_ref[...], h_ref[...]).astype(o_ref.dtype)


def _proj_t(h, wt, tm=1024, tn=1024):
    t, k = h.shape
    n = wt.shape[0]
    tm = min(tm, t)
    return pl.pallas_call(
        _proj_t_kernel,
        grid=(n // tn, t // tm),
        in_specs=[
            pl.BlockSpec((tn, k), lambda j, m: (j, 0)),
            pl.BlockSpec((tm, k), lambda j, m: (m, 0)),
        ],
        out_specs=pl.BlockSpec((tn, tm), lambda j, m: (j, m)),
        out_shape=jax.ShapeDtypeStruct((n, t), BF16),
        compiler_params=_params("parallel", "parallel"),
        name="proj_t",
    )(wt, h)


def _rms_heads(acc, gain_row, hd, scale, o_ref):
    for j in range(acc.shape[1] // hd):
        seg = acc[:, j * hd:(j + 1) * hd]
        ms = jnp.mean(seg * seg, axis=-1, keepdims=True)
        g = gain_row[:, j * hd:(j + 1) * hd]
        if scale != 1.0:
            g = g * scale
        o_ref[:, j * hd:(j + 1) * hd] = (seg * lax.rsqrt(ms + EPS) * g).astype(o_ref.dtype)


def _ep_dsa_q(acc, extras, outs):
    _rms_heads(acc, extras[0], DSA_LATENT, DSA_LATENT ** -0.5 * LOG2E, outs[0])


FOX_AUG = 2 * HEAD_DIM
FOX_BIAS_LANES = 3


def _rms_heads_aug(acc, gain_row, scale, extra_fn, o_ref):
    for j in range(acc.shape[1] // HEAD_DIM):
        cols = slice(j * HEAD_DIM, (j + 1) * HEAD_DIM)
        seg = acc[:, cols]
        ms = jnp.mean(seg * seg, axis=-1, keepdims=True)
        y = seg * lax.rsqrt(ms + EPS) * (gain_row[:, cols] * scale)
        o_ref[:, j * FOX_AUG:j * FOX_AUG + HEAD_DIM] = y.astype(o_ref.dtype)
        o_ref[:, j * FOX_AUG + HEAD_DIM:(j + 1) * FOX_AUG] = extra_fn(j).astype(o_ref.dtype)


def _ep_fox_q(acc, extras, outs):
    lane = lax.broadcasted_iota(jnp.int32, (acc.shape[0], HEAD_DIM), 1)
    ones = jnp.where(lane < FOX_BIAS_LANES, 1.0, 0.0)
    _rms_heads_aug(acc, extras[0], HEAD_DIM ** -0.5 * LOG2E, lambda j: ones, outs[0])


def _ep_fox_k(acc, extras, outs):
    gain_row, cum_t = extras
    heads = acc.shape[1] // HEAD_DIM
    first = pl.program_id(0) * heads
    r = lax.broadcasted_iota(jnp.int32, (FOX_HEADS, acc.shape[1]), 0)
    c = lax.broadcasted_iota(jnp.int32, (FOX_HEADS, acc.shape[1]), 1)
    head_col = jnp.where(r == first + c // HEAD_DIM, c % HEAD_DIM, -1)
    extra = jnp.zeros(acc.shape, F32)
    for i, term in enumerate(_split3(cum_t * (-LOG2E))):
        extra = extra + _dot(term, jnp.where(head_col == i, 1.0, 0.0).astype(BF16))
    _rms_heads_aug(acc, gain_row, 1.0,
                   lambda j: extra[:, j * HEAD_DIM:(j + 1) * HEAD_DIM], outs[0])


def _ep_cast(acc, extras, outs):
    outs[0][...] = acc.astype(outs[0].dtype)


def _ep_sigmoid(acc, extras, outs):
    outs[0][...] = jax.nn.sigmoid(acc).astype(outs[0].dtype)


def _ep_silu(acc, extras, outs):
    outs[0][...] = _silu(acc).astype(outs[0].dtype)


def _ep_hgrn_q(acc, extras, outs):
    outs[0][...] = (_silu(acc) * (HEAD_DIM ** -0.5)).astype(outs[0].dtype)


def _ep_hgrn_logf(acc, extras, outs):
    lb = extras[0]
    outs[0][...] = jnp.log(lb + (1.0 - lb) * jax.nn.sigmoid(acc))


_DSA_S_LAT = 0
_DSA_S_QI = DSA_LATENT
_DSA_S_KE = _DSA_S_QI + IDX_HEADS * IDX_DIM
_DSA_S_KO = _DSA_S_KE + 2 * IDX_DIM
_DSA_S_WI = _DSA_S_KO + 2 * IDX_DIM
_DSA_S_N = _DSA_S_WI + 128


def _ep_dsa_small(acc, extras, outs):
    lat_ref, qi_ref, ke_ref, ko_ref, wi_ref = outs
    _rms_heads(acc[:, :DSA_LATENT], extras[0][:, :DSA_LATENT], DSA_LATENT, 1.0, lat_ref)
    qi_ref[...] = acc[:, _DSA_S_QI:_DSA_S_KE].astype(BF16)
    ke_ref[...] = acc[:, _DSA_S_KE:_DSA_S_KO].astype(BF16)
    ko_ref[...] = acc[:, _DSA_S_KO:_DSA_S_WI].astype(BF16)
    wi_ref[...] = acc[:, _DSA_S_WI:_DSA_S_WI + IDX_HEADS] * (IDX_HEADS ** -0.5 * IDX_DIM ** -0.5)


def _out_kernel(a_ref, w_ref, x_ref, g_ref, o_ref):
    o_ref[...] = x_ref[...] + g_ref[0] * _dot(a_ref[...], w_ref[0])


def _out_proj(a, w_all, layer, x2, gate, seq, tm, tn=1024):
    t, k = a.shape
    d = w_all.shape[2]
    bsz = gate.shape[0]
    per_seq = seq // tm
    return pl.pallas_call(
        _out_kernel,
        grid=(d // tn, t // tm),
        in_specs=[
            pl.BlockSpec((tm, k), lambda j, m: (m, 0)),
            pl.BlockSpec((1, k, tn), lambda j, m: (layer, 0, j)),
            pl.BlockSpec((tm, tn), lambda j, m: (m, j)),
            pl.BlockSpec((1, 1, tn), lambda j, m: (m // per_seq, 0, j)),
        ],
        out_specs=pl.BlockSpec((tm, tn), lambda j, m: (m, j)),
        out_shape=jax.ShapeDtypeStruct((t, d), F32),
        compiler_params=_params("parallel", "parallel"),
        name="out_proj",
    )(a, w_all, x2, gate.reshape(bsz, 1, d))


def _ffn_in_kernel(h_ref, wu_ref, wg_ref, cw_ref, cb_ref, a_ref, carry_ref, wub_ref, wgb_ref,
                   *, per_seq, rc, cc):
    m = pl.program_id(1)
    tm, tn = a_ref.shape

    @pl.when(m == 0)
    def _():
        wub_ref[...] = wu_ref[0].astype(BF16)
        wgb_ref[...] = wg_ref[0].astype(BF16)

    @pl.when(m % per_seq == 0)
    def _():
        carry_ref[...] = jnp.zeros_like(carry_ref)

    cw = cw_ref[0]
    cb = cb_ref[0]
    row = lax.broadcasted_iota(jnp.int32, (rc, cc), 0)
    for c in range(tn // cc):
        cols = slice(c * cc, (c + 1) * cc)
        prev = carry_ref[:, cols]
        for r in range(tm // rc):
            rows = slice(r * rc, (r + 1) * rc)
            h = h_ref[rows, :]
            u = _dot(h, wub_ref[:, cols])
            g = _dot(h, wgb_ref[:, cols])
            p1 = prev[7:8, :]
            p2 = prev[6:7, :]
            g1 = jnp.where(row == 0, p1, pltpu.roll(g, 1, axis=0))
            g2 = jnp.where(row == 0, p2, jnp.where(row == 1, p1, pltpu.roll(g, 2, axis=0)))
            conv = cw[0:1, cols] * g2 + cw[1:2, cols] * g1 + cw[2:3, cols] * g + cb[:, cols]
            a_ref[rows, cols] = (_silu(conv) * u).astype(BF16)
            prev = g[rc - 8:, :]
        carry_ref[:, cols] = prev


def _ffn_in(h, w_all, conv_w, conv_b, layer, seq, tm=2048, tn=512, rc=512, cc=256):
    t, k = h.shape
    f = w_all.shape[2] // 2
    nf = f // tn
    tm = min(tm, seq)
    per_seq = seq // tm
    rc = min(rc, tm)
    return pl.pallas_call(
        functools.partial(_ffn_in_kernel, per_seq=per_seq, rc=rc, cc=cc),
        grid=(nf, t // tm),
        in_specs=[
            pl.BlockSpec((tm, k), lambda j, m: (m, 0)),
            pl.BlockSpec((1, k, tn), lambda j, m: (layer, 0, j)),
            pl.BlockSpec((1, k, tn), lambda j, m: (layer, 0, j + nf)),
            pl.BlockSpec((1, CONV_WIDTH, tn), lambda j, m: (layer, 0, j)),
            pl.BlockSpec((1, 1, tn), lambda j, m: (layer, 0, j)),
        ],
        out_specs=pl.BlockSpec((tm, tn), lambda j, m: (m, j)),
        out_shape=jax.ShapeDtypeStruct((t, f), BF16),
        scratch_shapes=[
            pltpu.VMEM((8, tn), F32),
            pltpu.VMEM((k, tn), BF16),
            pltpu.VMEM((k, tn), BF16),
        ],
        compiler_params=_params("parallel", "arbitrary"),
        name="ffn_in",
    )(h, w_all, w_all, conv_w, conv_b)


def _dsa_select_kernel(qi_ref, ke_ref, ko_ref, wit_ref, bias_ref, key_ref, cut_ref, *, topk, ts):
    i = pl.program_id(1)
    tq = qi_ref.shape[0]
    seq = ke_ref.shape[0]
    n_tiles = ((i + 1) * tq + ts - 1) // ts
    qi = qi_ref[...]
    qit = jnp.concatenate([qi[:, 128 * j:128 * (j + 1)].astype(F32).T.astype(BF16)
                           for j in range(IDX_HEADS // 2)], axis=1)
    wit = wit_ref[...]
    tpos = i * tq + lax.broadcasted_iota(jnp.int32, (ts, tq), 1)
    first_hidden = (tpos // CHUNK + 1) * CHUNK
    row = lax.broadcasted_iota(jnp.int32, (ts, tq), 0)

    def score_tile(t, carry):
        k0 = pl.multiple_of(t * ts, ts)
        sce = _dot(ke_ref[pl.ds(k0, ts), :], qit)
        sco = _dot(ko_ref[pl.ds(k0, ts), :], qit)
        score = jnp.zeros((ts, tq), F32)
        for j in range(IDX_HEADS // 2):
            cols = slice(j * tq, (j + 1) * tq)
            score = score + wit[2 * j:2 * j + 1] * jnp.maximum(sce[:, cols], 0.0)
            score = score + wit[2 * j + 1:2 * j + 2] * jnp.maximum(sco[:, cols], 0.0)
        score = score + 0.0
        bits = pltpu.bitcast(score, jnp.int32)
        key = jnp.where(bits < 0, bits ^ jnp.int32(0x7FFFFFFF), bits)
        key_ref[pl.ds(k0, ts), :] = jnp.where(k0 + row < first_hidden, key, INT_MIN)
        return carry

    lax.fori_loop(0, n_tiles, score_tile, 0)
    kf = jnp.float32(topk)

    def count(hit_fn):
        def body(t, cnt8):
            k0 = pl.multiple_of(t * ts, ts)
            hit = hit_fn(key_ref[pl.ds(k0, ts), :], k0)
            return cnt8 + jnp.sum(hit.reshape(ts // 8, 8, tq), axis=0)
        cnt8 = lax.fori_loop(0, n_tiles, body, jnp.zeros((8, tq), F32))
        return jnp.sum(cnt8, axis=0, keepdims=True)

    def count_ge(cand):
        return count(lambda key, k0: jnp.where(key >= cand, 1.0, 0.0))

    visible = count_ge(jnp.full((1, tq), INT_MIN + 1, jnp.int32))
    c_pos = count_ge(jnp.zeros((1, tq), jnp.int32))
    thr0 = jnp.where(c_pos >= kf, 0, INT_MIN).astype(jnp.int32)
    cnt0 = jnp.where(c_pos >= kf, c_pos, jnp.float32(seq + 1))

    def bit_step(b, state):
        thr, cnt = state
        cand = thr + lax.shift_left(jnp.int32(1), jnp.int32(30) - b)
        c = count_ge(cand)
        take = c >= kf
        return jnp.where(take, cand, thr), jnp.where(take, c, cnt)

    thr, cnt = lax.fori_loop(0, 31, bit_step, (thr0, cnt0))

    cut_ref[...] = jnp.full((1, tq), seq, jnp.int32)
    tied = jnp.where(cnt > kf, jnp.where(visible > kf, 1.0, 0.0), 0.0)

    @pl.when(jnp.max(tied) > 0.0)
    def _():
        need = kf - count(lambda key, k0: jnp.where(key > thr, 1.0, 0.0))
        nbits = seq.bit_length()

        def idx_step(b, cut):
            cand = cut + lax.shift_left(jnp.int32(1), jnp.int32(nbits - 1) - b)
            c = count(lambda key, k0: jnp.where(key == thr, jnp.where(k0 + row < cand, 1.0, 0.0), 0.0))
            return jnp.where(c <= need, cand, cut)

        cut = lax.fori_loop(0, nbits, idx_step, jnp.zeros((1, tq), jnp.int32))
        cut_ref[...] = jnp.where(tied > 0.0, cut, seq)

    def write_tile(t, carry):
        k0 = pl.multiple_of(t * ts, ts)
        key = key_ref[pl.ds(k0, ts), :]
        tie_keep = jnp.where(k0 + row < cut_ref[...], 0.0, NEG_BIG)
        keep = jnp.where(key > thr, 0.0,
                         jnp.where(key == thr, jnp.where(key > INT_MIN, tie_keep, NEG_BIG), NEG_BIG))
        bias_ref[0, pl.ds(k0, ts), :] = keep.astype(BF16)
        return carry

    def fill_tile(t, carry):
        k0 = pl.multiple_of(t * ts, ts)
        bias_ref[0, pl.ds(k0, ts), :] = jnp.full((ts, tq), NEG_BIG, BF16)
        return carry

    lax.fori_loop(0, n_tiles, write_tile, 0)
    lax.fori_loop(n_tiles, seq // ts, fill_tile, 0)


def _dsa_select(qi, ke, ko, wit, bsz, seq, tq=256, ts=256):
    nq = seq // tq
    topk = min(TOPK_MAX, seq // 4)
    return pl.pallas_call(
        functools.partial(_dsa_select_kernel, topk=topk, ts=ts),
        grid=(bsz, nq),
        in_specs=[
            pl.BlockSpec((tq, IDX_HEADS * IDX_DIM), lambda b, i: (b * nq + i, 0)),
            pl.BlockSpec((seq, 2 * IDX_DIM), lambda b, i: (b, 0)),
            pl.BlockSpec((seq, 2 * IDX_DIM), lambda b, i: (b, 0)),
            pl.BlockSpec((IDX_HEADS, tq), lambda b, i: (0, b * nq + i)),
        ],
        out_specs=pl.BlockSpec((1, seq, tq), lambda b, i: (b, 0, i)),
        out_shape=jax.ShapeDtypeStruct((bsz, seq, seq), BF16),
        scratch_shapes=[pltpu.VMEM((seq, tq), jnp.int32), pltpu.VMEM((1, tq), jnp.int32)],
        compiler_params=_params("parallel", "parallel"),
        name="dsa_select",
    )(qi, ke, ko, wit)


def _dsa_attn_kernel(q_ref, lat_ref, latt_ref, bias_ref, o_ref, m_ref, acc_ref,
                     s0_ref, s1_ref, p0_ref, p1_ref, a0_ref, a1_ref, *, hg, ng, tk):
    i = pl.program_id(1)
    tq = q_ref.shape[0]
    r = DSA_LATENT
    last_tile = lat_ref.shape[0] // tk - 1
    n_pairs = ((i + 1) * tq + 2 * tk - 1) // (2 * tk)
    tpos = i * tq + lax.broadcasted_iota(jnp.int32, (tk, tq), 1)
    row = lax.broadcasted_iota(jnp.int32, (tk, tq), 0)
    for g0 in range(0, DSA_HEADS // hg, ng):
        groups = [range((g0 + n) * hg, (g0 + n + 1) * hg) for n in range(ng)]
        q4t = [jnp.concatenate([q_ref[:, h * r:(h + 1) * r].astype(F32).T.astype(BF16) for h in heads], axis=1)
               for heads in groups]

        def scores(j, s_ref):
            k0 = pl.multiple_of(j * tk, tk)
            lat = lat_ref[pl.ds(k0, tk), :]
            for n in range(ng):
                s_ref[n] = _dot(lat, q4t[n])

        def softmax(j, s_ref, p_ref, a_ref):
            k0 = pl.multiple_of(j * tk, tk)
            dist = jnp.abs(tpos - (k0 + row)).astype(F32)
            bias = bias_ref[0, pl.ds(k0, tk), :].astype(F32)
            for n, heads in enumerate(groups):
                for idx, h in enumerate(heads):
                    cs = slice(idx * tq, (idx + 1) * tq)
                    slope = LOG2E * 2.0 ** (-8.0 * (h + 1) / DSA_HEADS)
                    s = s_ref[n, :, cs] + (bias - slope * dist)
                    m_old = m_ref[n, :, cs]
                    m_new = jnp.maximum(m_old, jnp.max(s, axis=0, keepdims=True))
                    m_ref[n, :, cs] = m_new
                    p_ref[n, :, cs] = jnp.exp2(s - m_new).astype(BF16)
                    a_ref[n, :, cs] = jnp.exp2(m_old - m_new)

        def values(j, p_ref, a_ref):
            k0 = pl.multiple_of(j * tk, tk)
            latt = jnp.concatenate([latt_ref[:, pl.ds(k0, tk)], jnp.ones((ONES_ROWS, tk), BF16)], axis=0)
            pvs = [_dot(latt, p_ref[n]) for n in range(ng)]
            for n in range(ng):
                acc_ref[n] = a_ref[n] * acc_ref[n] + pvs[n]

        m_ref[...] = jnp.full(m_ref.shape, NEG_BIG, F32)
        acc_ref[...] = jnp.zeros(acc_ref.shape, F32)
        scores(0, s0_ref)

        def trip(ta, first):
            scores(ta + 1, s1_ref)
            if not first:
                values(ta - 1, p1_ref, a1_ref)
            softmax(ta, s0_ref, p0_ref, a0_ref)
            scores(jnp.minimum(ta + 2, last_tile), s0_ref)
            values(ta, p0_ref, a0_ref)
            softmax(ta + 1, s1_ref, p1_ref, a1_ref)

        def body(jj, carry):
            trip(2 * jj, False)
            return carry

        trip(0, True)
        lax.fori_loop(1, n_pairs, body, 0)
        values(2 * n_pairs - 1, p1_ref, a1_ref)
        for n, heads in enumerate(groups):
            out = (acc_ref[n, :r, :] / acc_ref[n, r:r + 1, :]).T
            for idx, h in enumerate(heads):
                o_ref[:, h * r:(h + 1) * r] = out[idx * tq:(idx + 1) * tq].astype(BF16)


def _dsa_attn(q, lat, latt, bias, bsz, seq, tq=128, hg=4, ng=4, tk=256):
    nq = seq // tq
    width = DSA_HEADS * DSA_LATENT
    cols = hg * tq
    assert (seq // tk) % 2 == 0
    return pl.pallas_call(
        functools.partial(_dsa_attn_kernel, hg=hg, ng=ng, tk=tk),
        grid=(bsz, nq),
        in_specs=[
            pl.BlockSpec((tq, width), lambda b, i: (b * nq + i, 0)),
            pl.BlockSpec((seq, DSA_LATENT), lambda b, i: (b, 0)),
            pl.BlockSpec((DSA_LATENT, seq), lambda b, i: (0, b)),
            pl.BlockSpec((1, seq, tq), lambda b, i: (b, 0, i)),
        ],
        out_specs=pl.BlockSpec((tq, width), lambda b, i: (b * nq + i, 0)),
        out_shape=jax.ShapeDtypeStruct((bsz * seq, width), BF16),
        scratch_shapes=[
            pltpu.VMEM((ng, 1, cols), F32),
            pltpu.VMEM((ng, DSA_LATENT + ONES_ROWS, cols), F32),
            pltpu.VMEM((ng, tk, cols), F32),
            pltpu.VMEM((ng, tk, cols), F32),
            pltpu.VMEM((ng, tk, cols), BF16),
            pltpu.VMEM((ng, tk, cols), BF16),
            pltpu.VMEM((ng, 1, cols), F32),
            pltpu.VMEM((ng, 1, cols), F32),
        ],
        compiler_params=_params("parallel", "parallel"),
        name="dsa_attn",
    )(q, lat, latt, bias)


def _split3(x):
    hi = x.astype(BF16)
    r1 = x - hi.astype(F32)
    mid = r1.astype(BF16)
    lo = (r1 - mid.astype(F32)).astype(BF16)
    return hi, mid, lo


def _fox_cum_kernel(h_ref, wt_ref, b_ref, cum_ref, *, blk):
    fl = _dot_nt(wt_ref[...], h_ref[...]) + b_ref[...]
    logf = jax.nn.log_sigmoid(fl)
    seq = logf.shape[1]
    r = lax.broadcasted_iota(jnp.int32, (blk, blk), 0)
    c = lax.broadcasted_iota(jnp.int32, (blk, blk), 1)
    upper = jnp.where(r <= c, 1.0, 0.0).astype(BF16)
    carry = jnp.zeros((logf.shape[0], 1), F32)
    for j in range(seq // blk):
        hi, mid, lo = _split3(logf[:, j * blk:(j + 1) * blk])
        cs = (_dot(hi, upper) + _dot(mid, upper)) + _dot(lo, upper) + carry
        cum_ref[0, :, j * blk:(j + 1) * blk] = cs
        carry = cs[:, blk - 1:blk]


def _fox_cum(h, w_fl_t, b_f, bsz, seq, blk=256):
    d = h.shape[1]
    return pl.pallas_call(
        functools.partial(_fox_cum_kernel, blk=blk),
        grid=(bsz,),
        in_specs=[
            pl.BlockSpec((seq, d), lambda b: (b, 0)),
            pl.BlockSpec((FOX_HEADS, d), lambda b: (0, 0)),
            pl.BlockSpec((FOX_HEADS, 1), lambda b: (0, 0)),
        ],
        out_specs=pl.BlockSpec((1, FOX_HEADS, seq), lambda b: (b, 0, 0)),
        out_shape=jax.ShapeDtypeStruct((bsz, FOX_HEADS, seq), F32),
        compiler_params=_params("parallel"),
        name="fox_cum",
    )(h, w_fl_t, b_f.reshape(FOX_HEADS, 1))


def _fox_attn_kernel(q_ref, k_ref, vt_ref, g_ref, o_ref, m_ref, acc_ref):
    i = pl.program_id(2)
    tq = q_ref.shape[0]
    hb = g_ref.shape[1] // HEAD_DIM
    qts = [q_ref[:, h * FOX_AUG:(h + 1) * FOX_AUG].astype(F32).T.astype(BF16) for h in range(hb)]
    m_ref[...] = jnp.full(m_ref.shape, NEG_BIG, F32)
    acc_ref[...] = jnp.zeros(acc_ref.shape, F32)
    ones = jnp.ones((ONES_ROWS, tq), BF16)

    def tile(j, diagonal):
        k0 = pl.multiple_of(j * tq, tq)
        ss = [_dot(k_ref[pl.ds(k0, tq), h * FOX_AUG:(h + 1) * FOX_AUG], qts[h])
              for h in range(hb)]
        ps, alphas = [], []
        for h in range(hb):
            s = ss[h]
            if diagonal:
                key = lax.broadcasted_iota(jnp.int32, (tq, tq), 0)
                qry = lax.broadcasted_iota(jnp.int32, (tq, tq), 1)
                s = jnp.where(key <= qry, s, NEG_BIG)
            m_old = m_ref[h]
            m_new = jnp.maximum(m_old, jnp.max(s, axis=0, keepdims=True))
            m_ref[h] = m_new
            ps.append(jnp.exp2(s - m_new).astype(BF16))
            alphas.append(jnp.exp2(m_old - m_new))
        pvs = [_dot(jnp.concatenate([vt_ref[h * HEAD_DIM:(h + 1) * HEAD_DIM, pl.ds(k0, tq)], ones], axis=0),
                    ps[h]) for h in range(hb)]
        for h in range(hb):
            acc_ref[h] = alphas[h] * acc_ref[h] + pvs[h]

    def body(j, carry):
        tile(j, False)
        return carry

    lax.fori_loop(0, i, body, 0)
    tile(i, True)
    for h in range(hb):
        cols = slice(h * HEAD_DIM, (h + 1) * HEAD_DIM)
        o = (acc_ref[h, :HEAD_DIM, :] / acc_ref[h, HEAD_DIM:HEAD_DIM + 1, :]).T
        o_ref[:, cols] = (o * g_ref[:, cols].astype(F32)).astype(BF16)


def _fox_attn(q, k, vt, gate, bsz, seq, tq=256, hb=8):
    nq = seq // tq
    d = gate.shape[1]
    qmap = lambda b, h, i: (b * nq + i, h)
    return pl.pallas_call(
        _fox_attn_kernel,
        grid=(bsz, FOX_HEADS // hb, nq),
        in_specs=[
            pl.BlockSpec((tq, hb * FOX_AUG), qmap),
            pl.BlockSpec((seq, hb * FOX_AUG), lambda b, h, i: (b, h)),
            pl.BlockSpec((hb * HEAD_DIM, seq), lambda b, h, i: (h, b)),
            pl.BlockSpec((tq, hb * HEAD_DIM), qmap),
        ],
        out_specs=pl.BlockSpec((tq, hb * HEAD_DIM), qmap),
        out_shape=jax.ShapeDtypeStruct((bsz * seq, d), BF16),
        scratch_shapes=[
            pltpu.VMEM((hb, 1, tq), F32),
            pltpu.VMEM((hb, HEAD_DIM + ONES_ROWS, tq), F32),
        ],
        compiler_params=_params("parallel", "parallel", "parallel"),
        name="fox_attn",
    )(q, k, vt, gate)


HGRN_SUB = 8


def _hgrn_kernel(q_ref, lg_ref, v_ref, gate_ref, gain_ref, o_ref,
                 state_ref, oacc_ref, kpad_ref, fpad_ref, vpad_ref, *, hb):
    c = pl.program_id(2)
    cl = q_ref.shape[0]
    dk = HEAD_DIM
    sub = HGRN_SUB

    @pl.when(c == 0)
    def _():
        state_ref[...] = jnp.zeros_like(state_ref)
        kpad_ref[0:sub, :] = jnp.zeros((sub, dk), F32)
        fpad_ref[0:sub, :] = jnp.zeros((sub, dk), F32)
        vpad_ref[0:sub, :] = jnp.zeros((sub, dk), F32)

    r = lax.broadcasted_iota(jnp.int32, (cl, cl), 0)
    cc = lax.broadcasted_iota(jnp.int32, (cl, cl), 1)
    lower = jnp.where(cc <= r, 1.0, 0.0).astype(BF16)
    block_start = lax.broadcasted_iota(jnp.int32, (cl, dk), 0) % sub == 0

    for hh in range(hb):
        cols = slice(hh * dk, (hh + 1) * dk)
        q = q_ref[:, cols].astype(F32)
        lg = lg_ref[:, cols]
        vb = v_ref[:, cols]
        v = vb.astype(F32)
        hi, mid, lo = _split3(lg)
        gcum = (_dot(lower, hi) + _dot(lower, mid)) + _dot(lower, lo)
        f = jnp.exp(lg)
        k = 1.0 - f
        st = state_ref[hh]

        oacc_ref[...] = _dot_nt((q * jnp.exp(gcum)).astype(BF16), st.astype(BF16))

        m = cl // 2
        while m >= sub:
            for blk in range(cl // (2 * m)):
                r0 = blk * 2 * m
                mid_row = r0 + m
                gm = gcum[mid_row - 1:mid_row, :]
                qs = (q[mid_row:mid_row + m] * jnp.exp(gcum[mid_row:mid_row + m] - gm)).astype(BF16)
                ks = (k[r0:mid_row] * jnp.exp(gm - gcum[r0:mid_row])).astype(BF16)
                sc = _dot_nt(qs, ks)
                oacc_ref[mid_row:mid_row + m, :] += _dot(sc.astype(BF16), vb[r0:mid_row])
            m //= 2

        kpad_ref[sub:, :] = k
        fpad_ref[sub:, :] = jnp.where(block_start, 0.0, f)
        vpad_ref[sub:, :] = v
        diag = jnp.sum(q * k, axis=-1, keepdims=True) * v
        dec = None
        for delta in range(1, sub):
            fd = fpad_ref[sub - delta + 1:sub - delta + 1 + cl, :]
            dec = fd if dec is None else dec * fd
            kd = kpad_ref[sub - delta:sub - delta + cl, :]
            vd = vpad_ref[sub - delta:sub - delta + cl, :]
            diag = diag + jnp.sum(q * kd * dec, axis=-1, keepdims=True) * vd
        o = oacc_ref[...] + diag

        gl = gcum[cl - 1:cl, :]
        kdec = (k * jnp.exp(gl - gcum)).astype(BF16)
        state_ref[hh] = st * jnp.exp(gl) + _dot_tn(vb, kdec)

        ms = jnp.mean(o * o, axis=-1, keepdims=True)
        y = o * lax.rsqrt(ms + EPS) * gain_ref[...]
        o_ref[:, cols] = (y * gate_ref[:, cols].astype(F32)).astype(BF16)


def _hgrn_core(q, lg, v, gate, gain, bsz, seq, cl=256, hb=8):
    nc = seq // cl
    d = q.shape[1]
    blk = lambda b, g, c: (b * nc + c, g)
    spec = pl.BlockSpec((cl, hb * HEAD_DIM), blk)
    return pl.pallas_call(
        functools.partial(_hgrn_kernel, hb=hb),
        grid=(bsz, HGRN_HEADS // hb, nc),
        in_specs=[spec, spec, spec, spec, pl.BlockSpec((1, HEAD_DIM), lambda b, g, c: (0, 0))],
        out_specs=spec,
        out_shape=jax.ShapeDtypeStruct((bsz * seq, d), BF16),
        scratch_shapes=[
            pltpu.VMEM((hb, HEAD_DIM, HEAD_DIM), F32),
            pltpu.VMEM((cl, HEAD_DIM), F32),
            pltpu.VMEM((cl + HGRN_SUB, HEAD_DIM), F32),
            pltpu.VMEM((cl + HGRN_SUB, HEAD_DIM), F32),
            pltpu.VMEM((cl + HGRN_SUB, HEAD_DIM), F32),
        ],
        compiler_params=_params("parallel", "parallel", "arbitrary"),
        name="hgrn_core",
    )(q, lg, v, gate, gain.reshape(1, HEAD_DIM))


def _dsa_mixer(h, w_in_t, layer, q_gain, kv_gain, bsz, seq):
    d = h.shape[1]
    nq = DSA_HEADS * DSA_LATENT
    n_qi = IDX_HEADS * IDX_DIM
    small = w_in_t[layer, nq:]
    lat_w = small[:DSA_LATENT]
    qi_w = small[DSA_LATENT:DSA_LATENT + n_qi]
    ki_w = small[DSA_LATENT + n_qi:DSA_LATENT + n_qi + IDX_DIM]
    wi_w = small[DSA_LATENT + n_qi + IDX_DIM:]
    z = jnp.zeros((IDX_DIM, d), F32)
    w_s = jnp.concatenate(
        [lat_w, qi_w, ki_w, z, z, ki_w, wi_w, jnp.zeros((128 - IDX_HEADS, d), F32)], axis=0).astype(BF16).T
    q_gain_row = jnp.tile(q_gain, DSA_HEADS).reshape(1, nq)
    kv_gain_row = jnp.zeros((1, _DSA_S_N), F32).at[0, :DSA_LATENT].set(kv_gain)

    (q,) = _proj(h, w_in_t, 0, nq, _ep_dsa_q, [(1024, nq, BF16)], extras=(q_gain_row,), layer=layer)
    lat, qi, ke, ko, wi = _proj(
        h, w_s, 0, _DSA_S_N, _ep_dsa_small,
        [(DSA_LATENT, DSA_LATENT, BF16), (IDX_HEADS * IDX_DIM, IDX_HEADS * IDX_DIM, BF16),
         (2 * IDX_DIM, 2 * IDX_DIM, BF16), (2 * IDX_DIM, 2 * IDX_DIM, BF16), (IDX_HEADS, IDX_HEADS, F32)],
        extras=(kv_gain_row,), tn=_DSA_S_N)
    bias = _dsa_select(qi, ke, ko, wi.T, bsz, seq)
    return _dsa_attn(q, lat, lat.T, bias, bsz, seq)


def _fox_mixer(h, w_in_t, layer, b_f, q_gain, k_gain, bsz, seq):
    d = h.shape[1]
    w_fl_t = w_in_t[layer, 4 * d:].astype(BF16)
    qg = jnp.tile(q_gain, FOX_HEADS).reshape(1, d)
    kg = jnp.tile(k_gain, FOX_HEADS).reshape(1, d)
    cum = _fox_cum(h, w_fl_t, b_f, bsz, seq)
    cum_t = cum.transpose(0, 2, 1).reshape(bsz * seq, FOX_HEADS)
    aug = [(1024 // HEAD_DIM * FOX_AUG, FOX_HEADS * FOX_AUG, BF16)]
    (q,) = _proj(h, w_in_t, 0, d, _ep_fox_q, aug, extras=(qg,), layer=layer)
    (k,) = _proj(h, w_in_t, d, d, _ep_fox_k, aug, extras=(kg,), row_extras=(cum_t,), layer=layer)
    vt = _proj_t(h, w_in_t[layer, 2 * d:3 * d].astype(BF16))
    (gate,) = _proj(h, w_in_t, 3 * d, d, _ep_sigmoid, [(1024, d, BF16)], layer=layer)
    return _fox_attn(q, k, vt, gate, bsz, seq)


def _hgrn_mixer(h, w_in, lb, o_gain, bsz, seq):
    d = h.shape[1]
    w = w_in
    (q,) = _proj(h, w, 0, d, _ep_hgrn_q, [(1024, d, BF16)])
    (lg,) = _proj(h, w, d, d, _ep_hgrn_logf, [(1024, d, F32)], extras=(lb.reshape(1, d),))
    (v,) = _proj(h, w, 2 * d, d, _ep_cast, [(1024, d, BF16)])
    (gate,) = _proj(h, w, 3 * d, d, _ep_silu, [(1024, d, BF16)])
    return _hgrn_core(q, lg, v, gate, o_gain, bsz, seq)


def kernel(x, c, ada_w, ada_b, norm_mix_g, norm_ffn_g, dsa_w_in, dsa_q_norm, dsa_kv_norm, dsa_w_out, fox_w_in, fox_b_f, fox_q_norm, fox_k_norm, fox_w_out, hgrn_w_in, hgrn_lb, hgrn_o_norm, hgrn_w_out, ffn_w_in, ffn_conv_w, ffn_conv_b, ffn_w_out):
    bsz, seq, d = x.shape
    depth = ada_w.shape[0]
    f = ffn_conv_b.shape[1]

    mod = _ada_all(c, ada_w, ada_b)
    lb_soft = jax.nn.softmax(hgrn_lb.astype(F32), axis=0)
    lb_all = jnp.cumsum(lb_soft, axis=0) - lb_soft[0]
    conv_b = ffn_conv_b.reshape(depth, 1, f)

    w_outs = {0: dsa_w_out.astype(BF16), 1: fox_w_out.astype(BF16), 2: hgrn_w_out.astype(BF16)}
    ffn_w_out_b = ffn_w_out.astype(BF16)
    dsa_w_in_t = jnp.swapaxes(dsa_w_in, 1, 2)
    fox_w_in_t = jnp.swapaxes(fox_w_in, 1, 2)

    x2 = x.reshape(bsz * seq, d)
    for i in range(depth):
        sh1, sc1, g1, sh2, sc2, g2 = [mod[i, :, j * d:(j + 1) * d] for j in range(6)]
        h = _norm_mod(x2, norm_mix_g[i], sh1, sc1, seq)
        kind, j = i % 3, i // 3
        if kind == 0:
            a = _dsa_mixer(h, dsa_w_in_t, j, dsa_q_norm[j], dsa_kv_norm[j], bsz, seq)
        elif kind == 1:
            a = _fox_mixer(h, fox_w_in_t, j, fox_b_f[j], fox_q_norm[j], fox_k_norm[j], bsz, seq)
        else:
            a = _hgrn_mixer(h, hgrn_w_in[j], lb_all[i], hgrn_o_norm[j], bsz, seq)
        x2 = _out_proj(a, w_outs[kind], j, x2, g1, seq, tm=1024)
        h = _norm_mod(x2, norm_ffn_g[i], sh2, sc2, seq)
        a = _ffn_in(h, ffn_w_in, ffn_conv_w, conv_b, i, seq)
        x2 = _out_proj(a, ffn_w_out_b, i, x2, g2, seq, tm=512)
    return x2.reshape(bsz, seq, d)
```

```python
import functools

import jax
import jax.numpy as jnp
from jax import lax
from jax.experimental import pallas as pl
from jax.experimental.pallas import tpu as pltpu

F32 = jnp.float32
BF16 = jnp.bfloat16

EPS = 1e-6
CHUNK = 64
TOPK_MAX = 256

DSA_HEADS = 16
DSA_LATENT = 256
IDX_HEADS = 16
IDX_DIM = 64
FOX_HEADS = 16
HGRN_HEADS = 16
HEAD_DIM = 128
CONV_WIDTH = 3

NEG_BIG = -1e30
LOG2E = 1.4426950408889634
ONES_ROWS = 16
INT_MIN = -2 ** 31

V7X_VMEM_BYTES = 64 * 1024 * 1024
VMEM_LIMIT = 56 * 1024 * 1024


def _params(*sem):
    return pltpu.CompilerParams(dimension_semantics=sem, vmem_limit_bytes=VMEM_LIMIT)


def _dot(a, b):
    return jnp.dot(a, b, preferred_element_type=F32)


def _dot_nt(a, b):
    return lax.dot_general(a, b, (((1,), (1,)), ((), ())), preferred_element_type=F32)


def _dot_tn(a, b):
    return lax.dot_general(a, b, (((0,), (0,)), ((), ())), preferred_element_type=F32)


def _silu(x):
    return x * jax.nn.sigmoid(x)


def _ada_kernel(c_ref, w_ref, b_ref, o_ref):
    cond = _silu(c_ref[...]).astype(BF16)
    o_ref[0] = _dot(cond, w_ref[0].astype(BF16)) + b_ref[0]


def _ada_all(c, ada_w, ada_b, tn=1024):
    depth, d, n = ada_w.shape
    bsz = c.shape[0]
    rows = 8
    c_pad = jnp.zeros((rows, d), F32).at[:bsz].set(c)
    out = pl.pallas_call(
        _ada_kernel,
        grid=(depth, n // tn),
        in_specs=[
            pl.BlockSpec((rows, d), lambda l, j: (0, 0)),
            pl.BlockSpec((1, d, tn), lambda l, j: (l, 0, j)),
            pl.BlockSpec((1, 1, tn), lambda l, j: (l, 0, j)),
        ],
        out_specs=pl.BlockSpec((1, rows, tn), lambda l, j: (l, 0, j)),
        out_shape=jax.ShapeDtypeStruct((depth, rows, n), F32),
        compiler_params=_params("parallel", "parallel"),
        name="ada_mod",
    )(c_pad, ada_w, ada_b.reshape(depth, 1, n))
    return out[:, :bsz]


def _norm_mod_kernel(x_ref, g_ref, sh_ref, sc_ref, h_ref):
    x = x_ref[...]
    ms = jnp.mean(x * x, axis=-1, keepdims=True)
    y = x * lax.rsqrt(ms + EPS) * g_ref[...]
    h_ref[...] = (y * (1.0 + sc_ref[0]) + sh_ref[0]).astype(BF16)


def _norm_mod(x2, g, shift, scale, seq, tm=512):
    t, d = x2.shape
    per_seq = seq // tm
    bsz = shift.shape[0]
    vec = lambda m: (m // per_seq, 0, 0)
    return pl.pallas_call(
        _norm_mod_kernel,
        grid=(t // tm,),
        in_specs=[
            pl.BlockSpec((tm, d), lambda m: (m, 0)),
            pl.BlockSpec((1, d), lambda m: (0, 0)),
            pl.BlockSpec((1, 1, d), vec),
            pl.BlockSpec((1, 1, d), vec),
        ],
        out_specs=pl.BlockSpec((tm, d), lambda m: (m, 0)),
        out_shape=jax.ShapeDtypeStruct((t, d), BF16),
        compiler_params=_params("parallel"),
        name="norm_mod",
    )(x2, g.reshape(1, d), shift.reshape(bsz, 1, d), scale.reshape(bsz, 1, d))


def _proj_kernel(*refs, epilogue, n_extra, cast_w, w_t):
    h_ref, w_ref = refs[0], refs[1]
    extras = refs[2:2 + n_extra]
    if cast_w:
        outs, wb_ref = refs[2 + n_extra:-1], refs[-1]

        @pl.when(pl.program_id(1) == 0)
        def _():
            w32 = w_ref[0].T if w_t else w_ref[...]
            wb_ref[...] = w32.astype(BF16)

        w = wb_ref[...]
    else:
        outs = refs[2 + n_extra:]
        w = w_ref[...]
    acc = _dot(h_ref[...], w)
    epilogue(acc, [e[...] for e in extras], outs)


def _proj(h, w, col0, n, epilogue, out_defs, extras=(), row_extras=(), tm=1024, tn=1024, layer=None):
    t, k = h.shape
    tn = min(tn, n)
    tm = min(tm, t)
    assert n % tn == 0 and col0 % tn == 0 and t % tm == 0
    off = col0 // tn
    in_specs = [
        pl.BlockSpec((tm, k), lambda j, m: (m, 0)),
        pl.BlockSpec((k, tn), lambda j, m: (0, j + off)) if layer is None else
        pl.BlockSpec((1, tn, k), lambda j, m: (layer, j + off, 0)),
    ] + [pl.BlockSpec((1, tn), lambda j, m: (0, j)) for _ in extras
         ] + [pl.BlockSpec((tm, r.shape[1]), lambda j, m: (m, 0)) for r in row_extras]
    extras = tuple(extras) + tuple(row_extras)
    out_specs = [pl.BlockSpec((tm, wt), lambda j, m: (m, j)) for wt, _, _ in out_defs]
    out_shape = [jax.ShapeDtypeStruct((t, wtot), dt) for _, wtot, dt in out_defs]
    cast_w = w.dtype != BF16
    return pl.pallas_call(
        functools.partial(_proj_kernel, epilogue=epilogue, n_extra=len(extras), cast_w=cast_w,
                          w_t=layer is not None),
        grid=(n // tn, t // tm),
        in_specs=in_specs,
        out_specs=out_specs,
        out_shape=out_shape,
        scratch_shapes=[pltpu.VMEM((k, tn), BF16)] if cast_w else [],
        compiler_params=_params("parallel", "arbitrary"),
        name="proj_" + epilogue.__name__.strip("_"),
    )(h, w, *extras)


def _proj_t_kernel(wt_ref, h_ref, o_ref):
    o_ref[...] = _dot_nt(wt_ref[...], h_ref[...]).astype(o_ref.dtype)


def _proj_t(h, wt, tm=1024, tn=1024):
    t, k = h.shape
    n = wt.shape[0]
    tm = min(tm, t)
    return pl.pallas_call(
        _proj_t_kernel,
        grid=(n // tn, t // tm),
        in_specs=[
            pl.BlockSpec((tn, k), lambda j, m: (j, 0)),
            pl.BlockSpec((tm, k), lambda j, m: (m, 0)),
        ],
        out_specs=pl.BlockSpec((tn, tm), lambda j, m: (j, m)),
        out_shape=jax.ShapeDtypeStruct((n, t), BF16),
        compiler_params=_params("parallel", "parallel"),
        name="proj_t",
    )(wt, h)


def _rms_heads(acc, gain_row, hd, scale, o_ref):
    for j in range(acc.shape[1] // hd):
        seg = acc[:, j * hd:(j + 1) * hd]
        ms = jnp.mean(seg * seg, axis=-1, keepdims=True)
        g = gain_row[:, j * hd:(j + 1) * hd]
        if scale != 1.0:
            g = g * scale
        o_ref[:, j * hd:(j + 1) * hd] = (seg * lax.rsqrt(ms + EPS) * g).astype(o_ref.dtype)


def _ep_dsa_q(acc, extras, outs):
    _rms_heads(acc, extras[0], DSA_LATENT, DSA_LATENT ** -0.5 * LOG2E, outs[0])


FOX_AUG = 2 * HEAD_DIM
FOX_BIAS_LANES = 3


def _rms_heads_aug(acc, gain_row, scale, extra_fn, o_ref):
    for j in range(acc.shape[1] // HEAD_DIM):
        cols = slice(j * HEAD_DIM, (j + 1) * HEAD_DIM)
        seg = acc[:, cols]
        ms = jnp.mean(seg * seg, axis=-1, keepdims=True)
        y = seg * lax.rsqrt(ms + EPS) * (gain_row[:, cols] * scale)
        o_ref[:, j * FOX_AUG:j * FOX_AUG + HEAD_DIM] = y.astype(o_ref.dtype)
        o_ref[:, j * FOX_AUG + HEAD_DIM:(j + 1) * FOX_AUG] = extra_fn(j).astype(o_ref.dtype)


def _ep_fox_q(acc, extras, outs):
    lane = lax.broadcasted_iota(jnp.int32, (acc.shape[0], HEAD_DIM), 1)
    ones = jnp.where(lane < FOX_BIAS_LANES, 1.0, 0.0)
    _rms_heads_aug(acc, extras[0], HEAD_DIM ** -0.5 * LOG2E, lambda j: ones, outs[0])


def _ep_fox_k(acc, extras, outs):
    gain_row, cum_t = extras
    heads = acc.shape[1] // HEAD_DIM
    first = pl.program_id(0) * heads
    r = lax.broadcasted_iota(jnp.int32, (FOX_HEADS, acc.shape[1]), 0)
    c = lax.broadcasted_iota(jnp.int32, (FOX_HEADS, acc.shape[1]), 1)
    head_col = jnp.where(r == first + c // HEAD_DIM, c % HEAD_DIM, -1)
    extra = jnp.zeros(acc.shape, F32)
    for i, term in enumerate(_split3(cum_t * (-LOG2E))):
        extra = extra + _dot(term, jnp.where(head_col == i, 1.0, 0.0).astype(BF16))
    _rms_heads_aug(acc, gain_row, 1.0,
                   lambda j: extra[:, j * HEAD_DIM:(j + 1) * HEAD_DIM], outs[0])


def _ep_cast(acc, extras, outs):
    outs[0][...] = acc.astype(outs[0].dtype)


def _ep_sigmoid(acc, extras, outs):
    outs[0][...] = jax.nn.sigmoid(acc).astype(outs[0].dtype)


def _ep_silu(acc, extras, outs):
    outs[0][...] = _silu(acc).astype(outs[0].dtype)


def _ep_hgrn_q(acc, extras, outs):
    outs[0][...] = (_silu(acc) * (HEAD_DIM ** -0.5)).astype(outs[0].dtype)


def _ep_hgrn_logf(acc, extras, outs):
    lb = extras[0]
    outs[0][...] = jnp.log(lb + (1.0 - lb) * jax.nn.sigmoid(acc))


_DSA_S_LAT = 0
_DSA_S_QI = DSA_LATENT
_DSA_S_KE = _DSA_S_QI + IDX_HEADS * IDX_DIM
_DSA_S_KO = _DSA_S_KE + 2 * IDX_DIM
_DSA_S_WI = _DSA_S_KO + 2 * IDX_DIM
_DSA_S_N = _DSA_S_WI + 128


def _ep_dsa_small(acc, extras, outs):
    lat_ref, qi_ref, ke_ref, ko_ref, wi_ref = outs
    _rms_heads(acc[:, :DSA_LATENT], extras[0][:, :DSA_LATENT], DSA_LATENT, 1.0, lat_ref)
    qi_ref[...] = acc[:, _DSA_S_QI:_DSA_S_KE].astype(BF16)
    ke_ref[...] = acc[:, _DSA_S_KE:_DSA_S_KO].astype(BF16)
    ko_ref[...] = acc[:, _DSA_S_KO:_DSA_S_WI].astype(BF16)
    wi_ref[...] = acc[:, _DSA_S_WI:_DSA_S_WI + IDX_HEADS] * (IDX_HEADS ** -0.5 * IDX_DIM ** -0.5)


def _out_kernel(a_ref, w_ref, x_ref, g_ref, o_ref):
    o_ref[...] = x_ref[...] + g_ref[0] * _dot(a_ref[...], w_ref[0])


def _out_proj(a, w_all, layer, x2, gate, seq, tm, tn=1024):
    t, k = a.shape
    d = w_all.shape[2]
    bsz = gate.shape[0]
    per_seq = seq // tm
    return pl.pallas_call(
        _out_kernel,
        grid=(d // tn, t // tm),
        in_specs=[
            pl.BlockSpec((tm, k), lambda j, m: (m, 0)),
            pl.BlockSpec((1, k, tn), lambda j, m: (layer, 0, j)),
            pl.BlockSpec((tm, tn), lambda j, m: (m, j)),
            pl.BlockSpec((1, 1, tn), lambda j, m: (m // per_seq, 0, j)),
        ],
        out_specs=pl.BlockSpec((tm, tn), lambda j, m: (m, j)),
        out_shape=jax.ShapeDtypeStruct((t, d), F32),
        compiler_params=_params("parallel", "parallel"),
        name="out_proj",
    )(a, w_all, x2, gate.reshape(bsz, 1, d))


def _ffn_in_kernel(h_ref, wu_ref, wg_ref, cw_ref, cb_ref, wo_ref, a_ref, wob_ref,
                   carry_ref, wub_ref, wgb_ref, *, per_seq, rc, cc):
    m = pl.program_id(1)
    tm, tn = a_ref.shape

    @pl.when(m == 0)
    def _():
        wub_ref[...] = wu_ref[0].astype(BF16)
        wgb_ref[...] = wg_ref[0].astype(BF16)
        wob_ref[0] = wo_ref[0].astype(BF16)

    @pl.when(m % per_seq == 0)
    def _():
        carry_ref[...] = jnp.zeros_like(carry_ref)

    cw = cw_ref[0]
    cb = cb_ref[0]
    row = lax.broadcasted_iota(jnp.int32, (rc, cc), 0)
    for c in range(tn // cc):
        cols = slice(c * cc, (c + 1) * cc)
        prev = carry_ref[:, cols]
        for r in range(tm // rc):
            rows = slice(r * rc, (r + 1) * rc)
            h = h_ref[rows, :]
            u = _dot(h, wub_ref[:, cols])
            g = _dot(h, wgb_ref[:, cols])
            p1 = prev[7:8, :]
            p2 = prev[6:7, :]
            g1 = jnp.where(row == 0, p1, pltpu.roll(g, 1, axis=0))
            g2 = jnp.where(row == 0, p2, jnp.where(row == 1, p1, pltpu.roll(g, 2, axis=0)))
            conv = cw[0:1, cols] * g2 + cw[1:2, cols] * g1 + cw[2:3, cols] * g + cb[:, cols]
            a_ref[rows, cols] = (_silu(conv) * u).astype(BF16)
            prev = g[rc - 8:, :]
        carry_ref[:, cols] = prev


def _ffn_in(h, w_all, conv_w, conv_b, w_out_all, layer, seq, tm=2048, tn=512, rc=512, cc=256):
    t, k = h.shape
    f = w_all.shape[2] // 2
    nf = f // tn
    tm = min(tm, seq)
    per_seq = seq // tm
    rc = min(rc, tm)
    return pl.pallas_call(
        functools.partial(_ffn_in_kernel, per_seq=per_seq, rc=rc, cc=cc),
        grid=(nf, t // tm),
        in_specs=[
            pl.BlockSpec((tm, k), lambda j, m: (m, 0)),
            pl.BlockSpec((1, k, tn), lambda j, m: (layer, 0, j)),
            pl.BlockSpec((1, k, tn), lambda j, m: (layer, 0, j + nf)),
            pl.BlockSpec((1, CONV_WIDTH, tn), lambda j, m: (layer, 0, j)),
            pl.BlockSpec((1, 1, tn), lambda j, m: (layer, 0, j)),
            pl.BlockSpec((1, tn, k), lambda j, m: (layer, j, 0)),
        ],
        out_specs=[pl.BlockSpec((tm, tn), lambda j, m: (m, j)),
                   pl.BlockSpec((1, tn, k), lambda j, m: (0, j, 0))],
        out_shape=[jax.ShapeDtypeStruct((t, f), BF16), jax.ShapeDtypeStruct((1, f, k), BF16)],
        scratch_shapes=[
            pltpu.VMEM((8, tn), F32),
            pltpu.VMEM((k, tn), BF16),
            pltpu.VMEM((k, tn), BF16),
        ],
        compiler_params=_params("parallel", "arbitrary"),
        name="ffn_in",
    )(h, w_all, w_all, conv_w, conv_b, w_out_all)


def _dsa_select_kernel(qi_ref, ke_ref, ko_ref, wit_ref, bias_ref, key_ref, cut_ref, *, topk, ts):
    i = pl.program_id(1)
    tq = qi_ref.shape[0]
    seq = ke_ref.shape[0]
    n_tiles = ((i + 1) * tq + ts - 1) // ts
    qi = qi_ref[...]
    qit = jnp.concatenate([qi[:, 128 * j:128 * (j + 1)].astype(F32).T.astype(BF16)
                           for j in range(IDX_HEADS // 2)], axis=1)
    wit = wit_ref[...]
    tpos = i * tq + lax.broadcasted_iota(jnp.int32, (ts, tq), 1)
    first_hidden = (tpos // CHUNK + 1) * CHUNK
    row = lax.broadcasted_iota(jnp.int32, (ts, tq), 0)

    def score_tile(t, carry):
        k0 = pl.multiple_of(t * ts, ts)
        sce = _dot(ke_ref[pl.ds(k0, ts), :], qit)
        sco = _dot(ko_ref[pl.ds(k0, ts), :], qit)
        score = jnp.zeros((ts, tq), F32)
        for j in range(IDX_HEADS // 2):
            cols = slice(j * tq, (j + 1) * tq)
            score = score + wit[2 * j:2 * j + 1] * jnp.maximum(sce[:, cols], 0.0)
            score = score + wit[2 * j + 1:2 * j + 2] * jnp.maximum(sco[:, cols], 0.0)
        score = score + 0.0
        bits = pltpu.bitcast(score, jnp.int32)
        key = jnp.where(bits < 0, bits ^ jnp.int32(0x7FFFFFFF), bits)
        key_ref[pl.ds(k0, ts), :] = jnp.where(k0 + row < first_hidden, key, INT_MIN)
        return carry

    lax.fori_loop(0, n_tiles, score_tile, 0)
    kf = jnp.float32(topk)

    def count(hit_fn):
        def body(t, cnt8):
            k0 = pl.multiple_of(t * ts, ts)
            hit = hit_fn(key_ref[pl.ds(k0, ts), :], k0)
            return cnt8 + jnp.sum(hit.reshape(ts // 8, 8, tq), axis=0)
        cnt8 = lax.fori_loop(0, n_tiles, body, jnp.zeros((8, tq), F32))
        return jnp.sum(cnt8, axis=0, keepdims=True)

    def count_ge(cand):
        return count(lambda key, k0: jnp.where(key >= cand, 1.0, 0.0))

    visible = count_ge(jnp.full((1, tq), INT_MIN + 1, jnp.int32))
    c_pos = count_ge(jnp.zeros((1, tq), jnp.int32))
    thr0 = jnp.where(c_pos >= kf, 0, INT_MIN).astype(jnp.int32)
    cnt0 = jnp.where(c_pos >= kf, c_pos, jnp.float32(seq + 1))

    def bit_step(b, state):
        thr, cnt = state
        cand = thr + lax.shift_left(jnp.int32(1), jnp.int32(30) - b)
        c = count_ge(cand)
        take = c >= kf
        return jnp.where(take, cand, thr), jnp.where(take, c, cnt)

    thr, cnt = lax.fori_loop(0, 31, bit_step, (thr0, cnt0))

    cut_ref[...] = jnp.full((1, tq), seq, jnp.int32)
    tied = jnp.where(cnt > kf, jnp.where(visible > kf, 1.0, 0.0), 0.0)

    @pl.when(jnp.max(tied) > 0.0)
    def _():
        need = kf - count(lambda key, k0: jnp.where(key > thr, 1.0, 0.0))
        nbits = seq.bit_length()

        def idx_step(b, cut):
            cand = cut + lax.shift_left(jnp.int32(1), jnp.int32(nbits - 1) - b)
            c = count(lambda key, k0: jnp.where(key == thr, jnp.where(k0 + row < cand, 1.0, 0.0), 0.0))
            return jnp.where(c <= need, cand, cut)

        cut = lax.fori_loop(0, nbits, idx_step, jnp.zeros((1, tq), jnp.int32))
        cut_ref[...] = jnp.where(tied > 0.0, cut, seq)

    def write_tile(t, carry):
        k0 = pl.multiple_of(t * ts, ts)
        key = key_ref[pl.ds(k0, ts), :]
        tie_keep = jnp.where(k0 + row < cut_ref[...], 0.0, NEG_BIG)
        keep = jnp.where(key > thr, 0.0,
                         jnp.where(key == thr, jnp.where(key > INT_MIN, tie_keep, NEG_BIG), NEG_BIG))
        bias_ref[0, pl.ds(k0, ts), :] = keep.astype(BF16)
        return carry

    def fill_tile(t, carry):
        k0 = pl.multiple_of(t * ts, ts)
        bias_ref[0, pl.ds(k0, ts), :] = jnp.full((ts, tq), NEG_BIG, BF16)
        return carry

    lax.fori_loop(0, n_tiles, write_tile, 0)
    lax.fori_loop(n_tiles, seq // ts, fill_tile, 0)


def _dsa_select(qi, ke, ko, wit, bsz, seq, tq=256, ts=256):
    nq = seq // tq
    topk = min(TOPK_MAX, seq // 4)
    return pl.pallas_call(
        functools.partial(_dsa_select_kernel, topk=topk, ts=ts),
        grid=(bsz, nq),
        in_specs=[
            pl.BlockSpec((tq, IDX_HEADS * IDX_DIM), lambda b, i: (b * nq + i, 0)),
            pl.BlockSpec((seq, 2 * IDX_DIM), lambda b, i: (b, 0)),
            pl.BlockSpec((seq, 2 * IDX_DIM), lambda b, i: (b, 0)),
            pl.BlockSpec((IDX_HEADS, tq), lambda b, i: (0, b * nq + i)),
        ],
        out_specs=pl.BlockSpec((1, seq, tq), lambda b, i: (b, 0, i)),
        out_shape=jax.ShapeDtypeStruct((bsz, seq, seq), BF16),
        scratch_shapes=[pltpu.VMEM((seq, tq), jnp.int32), pltpu.VMEM((1, tq), jnp.int32)],
        compiler_params=_params("parallel", "parallel"),
        name="dsa_select",
    )(qi, ke, ko, wit)


def _dsa_attn_kernel(q_ref, lat_ref, latt_ref, bias_ref, o_ref, m_ref, acc_ref,
                     s0_ref, s1_ref, p0_ref, p1_ref, a0_ref, a1_ref, *, hg, ng, tk):
    i = pl.program_id(1)
    tq = q_ref.shape[0]
    r = DSA_LATENT
    last_tile = lat_ref.shape[0] // tk - 1
    n_pairs = ((i + 1) * tq + 2 * tk - 1) // (2 * tk)
    tpos = i * tq + lax.broadcasted_iota(jnp.int32, (tk, tq), 1)
    row = lax.broadcasted_iota(jnp.int32, (tk, tq), 0)
    for g0 in range(0, DSA_HEADS // hg, ng):
        groups = [range((g0 + n) * hg, (g0 + n + 1) * hg) for n in range(ng)]
        q4t = [jnp.concatenate([q_ref[:, h * r:(h + 1) * r].astype(F32).T.astype(BF16) for h in heads], axis=1)
               for heads in groups]

        def scores(j, s_ref):
            k0 = pl.multiple_of(j * tk, tk)
            lat = lat_ref[pl.ds(k0, tk), :]
            for n in range(ng):
                s_ref[n] = _dot(lat, q4t[n])

        def softmax(j, s_ref, p_ref, a_ref):
            k0 = pl.multiple_of(j * tk, tk)
            dist = jnp.abs(tpos - (k0 + row)).astype(F32)
            bias = bias_ref[0, pl.ds(k0, tk), :].astype(F32)
            for n, heads in enumerate(groups):
                for idx, h in enumerate(heads):
                    cs = slice(idx * tq, (idx + 1) * tq)
                    slope = LOG2E * 2.0 ** (-8.0 * (h + 1) / DSA_HEADS)
                    s = s_ref[n, :, cs] + (bias - slope * dist)
                    m_old = m_ref[n, :, cs]
                    m_new = jnp.maximum(m_old, jnp.max(s, axis=0, keepdims=True))
                    m_ref[n, :, cs] = m_new
                    p_ref[n, :, cs] = jnp.exp2(s - m_new).astype(BF16)
                    a_ref[n, :, cs] = jnp.exp2(m_old - m_new)

        def values(j, p_ref, a_ref):
            k0 = pl.multiple_of(j * tk, tk)
            latt = jnp.concatenate([latt_ref[:, pl.ds(k0, tk)], jnp.ones((ONES_ROWS, tk), BF16)], axis=0)
            pvs = [_dot(latt, p_ref[n]) for n in range(ng)]
            for n in range(ng):
                acc_ref[n] = a_ref[n] * acc_ref[n] + pvs[n]

        m_ref[...] = jnp.full(m_ref.shape, NEG_BIG, F32)
        acc_ref[...] = jnp.zeros(acc_ref.shape, F32)
        scores(0, s0_ref)

        def trip(ta, first):
            scores(ta + 1, s1_ref)
            if not first:
                values(ta - 1, p1_ref, a1_ref)
            softmax(ta, s0_ref, p0_ref, a0_ref)
            scores(jnp.minimum(ta + 2, last_tile), s0_ref)
            values(ta, p0_ref, a0_ref)
            softmax(ta + 1, s1_ref, p1_ref, a1_ref)

        def body(jj, carry):
            trip(2 * jj, False)
            return carry

        trip(0, True)
        lax.fori_loop(1, n_pairs, body, 0)
        values(2 * n_pairs - 1, p1_ref, a1_ref)
        for n, heads in enumerate(groups):
            out = (acc_ref[n, :r, :] / acc_ref[n, r:r + 1, :]).T
            for idx, h in enumerate(heads):
                o_ref[:, h * r:(h + 1) * r] = out[idx * tq:(idx + 1) * tq].astype(BF16)


def _dsa_attn(q, lat, latt, bias, bsz, seq, tq=128, hg=4, ng=4, tk=256):
    nq = seq // tq
    width = DSA_HEADS * DSA_LATENT
    cols = hg * tq
    assert (seq // tk) % 2 == 0
    return pl.pallas_call(
        functools.partial(_dsa_attn_kernel, hg=hg, ng=ng, tk=tk),
        grid=(bsz, nq),
        in_specs=[
            pl.BlockSpec((tq, width), lambda b, i: (b * nq + i, 0)),
            pl.BlockSpec((seq, DSA_LATENT), lambda b, i: (b, 0)),
            pl.BlockSpec((DSA_LATENT, seq), lambda b, i: (0, b)),
            pl.BlockSpec((1, seq, tq), lambda b, i: (b, 0, i)),
        ],
        out_specs=pl.BlockSpec((tq, width), lambda b, i: (b * nq + i, 0)),
        out_shape=jax.ShapeDtypeStruct((bsz * seq, width), BF16),
        scratch_shapes=[
            pltpu.VMEM((ng, 1, cols), F32),
            pltpu.VMEM((ng, DSA_LATENT + ONES_ROWS, cols), F32),
            pltpu.VMEM((ng, tk, cols), F32),
            pltpu.VMEM((ng, tk, cols), F32),
            pltpu.VMEM((ng, tk, cols), BF16),
            pltpu.VMEM((ng, tk, cols), BF16),
            pltpu.VMEM((ng, 1, cols), F32),
            pltpu.VMEM((ng, 1, cols), F32),
        ],
        compiler_params=_params("parallel", "parallel"),
        name="dsa_attn",
    )(q, lat, latt, bias)


def _split3(x):
    hi = x.astype(BF16)
    r1 = x - hi.astype(F32)
    mid = r1.astype(BF16)
    lo = (r1 - mid.astype(F32)).astype(BF16)
    return hi, mid, lo


def _fox_cum_kernel(h_ref, wt_ref, b_ref, cum_ref, *, blk):
    fl = _dot_nt(wt_ref[...], h_ref[...]) + b_ref[...]
    logf = jax.nn.log_sigmoid(fl)
    seq = logf.shape[1]
    r = lax.broadcasted_iota(jnp.int32, (blk, blk), 0)
    c = lax.broadcasted_iota(jnp.int32, (blk, blk), 1)
    upper = jnp.where(r <= c, 1.0, 0.0).astype(BF16)
    carry = jnp.zeros((logf.shape[0], 1), F32)
    for j in range(seq // blk):
        hi, mid, lo = _split3(logf[:, j * blk:(j + 1) * blk])
        cs = (_dot(hi, upper) + _dot(mid, upper)) + _dot(lo, upper) + carry
        cum_ref[0, :, j * blk:(j + 1) * blk] = cs
        carry = cs[:, blk - 1:blk]


def _fox_cum(h, w_fl_t, b_f, bsz, seq, blk=256):
    d = h.shape[1]
    return pl.pallas_call(
        functools.partial(_fox_cum_kernel, blk=blk),
        grid=(bsz,),
        in_specs=[
            pl.BlockSpec((seq, d), lambda b: (b, 0)),
            pl.BlockSpec((FOX_HEADS, d), lambda b: (0, 0)),
            pl.BlockSpec((FOX_HEADS, 1), lambda b: (0, 0)),
        ],
        out_specs=pl.BlockSpec((1, FOX_HEADS, seq), lambda b: (b, 0, 0)),
        out_shape=jax.ShapeDtypeStruct((bsz, FOX_HEADS, seq), F32),
        compiler_params=_params("parallel"),
        name="fox_cum",
    )(h, w_fl_t, b_f.reshape(FOX_HEADS, 1))


def _fox_attn_kernel(q_ref, k_ref, vt_ref, g_ref, o_ref, m_ref, acc_ref):
    i = pl.program_id(2)
    tq = q_ref.shape[0]
    hb = g_ref.shape[1] // HEAD_DIM
    qts = [q_ref[:, h * FOX_AUG:(h + 1) * FOX_AUG].astype(F32).T.astype(BF16) for h in range(hb)]
    m_ref[...] = jnp.full(m_ref.shape, NEG_BIG, F32)
    acc_ref[...] = jnp.zeros(acc_ref.shape, F32)
    ones = jnp.ones((ONES_ROWS, tq), BF16)

    def tile(j, diagonal):
        k0 = pl.multiple_of(j * tq, tq)
        ss = [_dot(k_ref[pl.ds(k0, tq), h * FOX_AUG:(h + 1) * FOX_AUG], qts[h])
              for h in range(hb)]
        ps, alphas = [], []
        for h in range(hb):
            s = ss[h]
            if diagonal:
                key = lax.broadcasted_iota(jnp.int32, (tq, tq), 0)
                qry = lax.broadcasted_iota(jnp.int32, (tq, tq), 1)
                s = jnp.where(key <= qry, s, NEG_BIG)
            m_old = m_ref[h]
            m_new = jnp.maximum(m_old, jnp.max(s, axis=0, keepdims=True))
            m_ref[h] = m_new
            ps.append(jnp.exp2(s - m_new).astype(BF16))
            alphas.append(jnp.exp2(m_old - m_new))
        pvs = [_dot(jnp.concatenate([vt_ref[h * HEAD_DIM:(h + 1) * HEAD_DIM, pl.ds(k0, tq)], ones], axis=0),
                    ps[h]) for h in range(hb)]
        for h in range(hb):
            acc_ref[h] = alphas[h] * acc_ref[h] + pvs[h]

    def body(j, carry):
        tile(j, False)
        return carry

    lax.fori_loop(0, i, body, 0)
    tile(i, True)
    for h in range(hb):
        cols = slice(h * HEAD_DIM, (h + 1) * HEAD_DIM)
        o = (acc_ref[h, :HEAD_DIM, :] / acc_ref[h, HEAD_DIM:HEAD_DIM + 1, :]).T
        o_ref[:, cols] = (o * g_ref[:, cols].astype(F32)).astype(BF16)


def _fox_attn(q, k, vt, gate, bsz, seq, tq=256, hb=8):
    nq = seq // tq
    d = gate.shape[1]
    qmap = lambda b, h, i: (b * nq + i, h)
    return pl.pallas_call(
        _fox_attn_kernel,
        grid=(bsz, FOX_HEADS // hb, nq),
        in_specs=[
            pl.BlockSpec((tq, hb * FOX_AUG), qmap),
            pl.BlockSpec((seq, hb * FOX_AUG), lambda b, h, i: (b, h)),
            pl.BlockSpec((hb * HEAD_DIM, seq), lambda b, h, i: (h, b)),
            pl.BlockSpec((tq, hb * HEAD_DIM), qmap),
        ],
        out_specs=pl.BlockSpec((tq, hb * HEAD_DIM), qmap),
        out_shape=jax.ShapeDtypeStruct((bsz * seq, d), BF16),
        scratch_shapes=[
            pltpu.VMEM((hb, 1, tq), F32),
            pltpu.VMEM((hb, HEAD_DIM + ONES_ROWS, tq), F32),
        ],
        compiler_params=_params("parallel", "parallel", "parallel"),
        name="fox_attn",
    )(q, k, vt, gate)


HGRN_SUB = 8


def _hgrn_kernel(q_ref, lg_ref, v_ref, gate_ref, gain_ref, o_ref,
                 state_ref, oacc_ref, kpad_ref, fpad_ref, vpad_ref, *, hb):
    c = pl.program_id(2)
    cl = q_ref.shape[0]
    dk = HEAD_DIM
    sub = HGRN_SUB

    @pl.when(c == 0)
    def _():
        state_ref[...] = jnp.zeros_like(state_ref)
        kpad_ref[0:sub, :] = jnp.zeros((sub, dk), F32)
        fpad_ref[0:sub, :] = jnp.zeros((sub, dk), F32)
        vpad_ref[0:sub, :] = jnp.zeros((sub, dk), F32)

    r = lax.broadcasted_iota(jnp.int32, (cl, cl), 0)
    cc = lax.broadcasted_iota(jnp.int32, (cl, cl), 1)
    lower = jnp.where(cc <= r, 1.0, 0.0).astype(BF16)
    block_start = lax.broadcasted_iota(jnp.int32, (cl, dk), 0) % sub == 0

    for hh in range(hb):
        cols = slice(hh * dk, (hh + 1) * dk)
        q = q_ref[:, cols].astype(F32)
        lg = lg_ref[:, cols]
        vb = v_ref[:, cols]
        v = vb.astype(F32)
        hi, mid, lo = _split3(lg)
        gcum = (_dot(lower, hi) + _dot(lower, mid)) + _dot(lower, lo)
        f = jnp.exp(lg)
        k = 1.0 - f
        st = state_ref[hh]

        oacc_ref[...] = _dot_nt((q * jnp.exp(gcum)).astype(BF16), st.astype(BF16))

        m = cl // 2
        while m >= sub:
            for blk in range(cl // (2 * m)):
                r0 = blk * 2 * m
                mid_row = r0 + m
                gm = gcum[mid_row - 1:mid_row, :]
                qs = (q[mid_row:mid_row + m] * jnp.exp(gcum[mid_row:mid_row + m] - gm)).astype(BF16)
                ks = (k[r0:mid_row] * jnp.exp(gm - gcum[r0:mid_row])).astype(BF16)
                sc = _dot_nt(qs, ks)
                oacc_ref[mid_row:mid_row + m, :] += _dot(sc.astype(BF16), vb[r0:mid_row])
            m //= 2

        kpad_ref[sub:, :] = k
        fpad_ref[sub:, :] = jnp.where(block_start, 0.0, f)
        vpad_ref[sub:, :] = v
        diag = jnp.sum(q * k, axis=-1, keepdims=True) * v
        dec = None
        for delta in range(1, sub):
            fd = fpad_ref[sub - delta + 1:sub - delta + 1 + cl, :]
            dec = fd if dec is None else dec * fd
            kd = kpad_ref[sub - delta:sub - delta + cl, :]
            vd = vpad_ref[sub - delta:sub - delta + cl, :]
            diag = diag + jnp.sum(q * kd * dec, axis=-1, keepdims=True) * vd
        o = oacc_ref[...] + diag

        gl = gcum[cl - 1:cl, :]
        kdec = (k * jnp.exp(gl - gcum)).astype(BF16)
        state_ref[hh] = st * jnp.exp(gl) + _dot_tn(vb, kdec)

        ms = jnp.mean(o * o, axis=-1, keepdims=True)
        y = o * lax.rsqrt(ms + EPS) * gain_ref[...]
        o_ref[:, cols] = (y * gate_ref[:, cols].astype(F32)).astype(BF16)


def _hgrn_core(q, lg, v, gate, gain, bsz, seq, cl=256, hb=8):
    nc = seq // cl
    d = q.shape[1]
    blk = lambda b, g, c: (b * nc + c, g)
    spec = pl.BlockSpec((cl, hb * HEAD_DIM), blk)
    return pl.pallas_call(
        functools.partial(_hgrn_kernel, hb=hb),
        grid=(bsz, HGRN_HEADS // hb, nc),
        in_specs=[spec, spec, spec, spec, pl.BlockSpec((1, HEAD_DIM), lambda b, g, c: (0, 0))],
        out_specs=spec,
        out_shape=jax.ShapeDtypeStruct((bsz * seq, d), BF16),
        scratch_shapes=[
            pltpu.VMEM((hb, HEAD_DIM, HEAD_DIM), F32),
            pltpu.VMEM((cl, HEAD_DIM), F32),
            pltpu.VMEM((cl + HGRN_SUB, HEAD_DIM), F32),
            pltpu.VMEM((cl + HGRN_SUB, HEAD_DIM), F32),
            pltpu.VMEM((cl + HGRN_SUB, HEAD_DIM), F32),
        ],
        compiler_params=_params("parallel", "parallel", "arbitrary"),
        name="hgrn_core",
    )(q, lg, v, gate, gain.reshape(1, HEAD_DIM))


def _dsa_mixer(h, w_in_t, layer, q_gain, kv_gain, bsz, seq):
    d = h.shape[1]
    nq = DSA_HEADS * DSA_LATENT
    n_qi = IDX_HEADS * IDX_DIM
    small = lax.optimization_barrier(w_in_t[layer, nq:])
    lat_w = small[:DSA_LATENT]
    qi_w = small[DSA_LATENT:DSA_LATENT + n_qi]
    ki_w = small[DSA_LATENT + n_qi:DSA_LATENT + n_qi + IDX_DIM]
    wi_w = small[DSA_LATENT + n_qi + IDX_DIM:]
    z = jnp.zeros((IDX_DIM, d), F32)
    w_s = jnp.concatenate(
        [lat_w, qi_w, ki_w, z, z, ki_w, wi_w, jnp.zeros((128 - IDX_HEADS, d), F32)], axis=0).astype(BF16).T
    q_gain_row = jnp.tile(q_gain, DSA_HEADS).reshape(1, nq)
    kv_gain_row = jnp.zeros((1, _DSA_S_N), F32).at[0, :DSA_LATENT].set(kv_gain)

    (q,) = _proj(h, w_in_t, 0, nq, _ep_dsa_q, [(1024, nq, BF16)], extras=(q_gain_row,), layer=layer)
    lat, qi, ke, ko, wi = _proj(
        h, w_s, 0, _DSA_S_N, _ep_dsa_small,
        [(DSA_LATENT, DSA_LATENT, BF16), (IDX_HEADS * IDX_DIM, IDX_HEADS * IDX_DIM, BF16),
         (2 * IDX_DIM, 2 * IDX_DIM, BF16), (2 * IDX_DIM, 2 * IDX_DIM, BF16), (IDX_HEADS, IDX_HEADS, F32)],
        extras=(kv_gain_row,), tn=_DSA_S_N)
    bias = _dsa_select(qi, ke, ko, wi.T, bsz, seq)
    return _dsa_attn(q, lat, lat.T, bias, bsz, seq)


def _fox_mixer(h, w_in_t, layer, b_f, q_gain, k_gain, bsz, seq):
    d = h.shape[1]
    w_fl_t = lax.optimization_barrier(w_in_t[layer, 4 * d:]).astype(BF16)
    qg = jnp.tile(q_gain, FOX_HEADS).reshape(1, d)
    kg = jnp.tile(k_gain, FOX_HEADS).reshape(1, d)
    cum = _fox_cum(h, w_fl_t, b_f, bsz, seq)
    cum_t = cum.transpose(0, 2, 1).reshape(bsz * seq, FOX_HEADS)
    aug = [(1024 // HEAD_DIM * FOX_AUG, FOX_HEADS * FOX_AUG, BF16)]
    (q,) = _proj(h, w_in_t, 0, d, _ep_fox_q, aug, extras=(qg,), layer=layer)
    (k,) = _proj(h, w_in_t, d, d, _ep_fox_k, aug, extras=(kg,), row_extras=(cum_t,), layer=layer)
    vt = _proj_t(h, lax.optimization_barrier(w_in_t[layer, 2 * d:3 * d]).astype(BF16))
    (gate,) = _proj(h, w_in_t, 3 * d, d, _ep_sigmoid, [(1024, d, BF16)], layer=layer)
    return _fox_attn(q, k, vt, gate, bsz, seq)


def _hgrn_mixer(h, w_in, lb, o_gain, bsz, seq):
    d = h.shape[1]
    w = w_in
    (q,) = _proj(h, w, 0, d, _ep_hgrn_q, [(1024, d, BF16)])
    (lg,) = _proj(h, w, d, d, _ep_hgrn_logf, [(1024, d, F32)], extras=(lb.reshape(1, d),))
    (v,) = _proj(h, w, 2 * d, d, _ep_cast, [(1024, d, BF16)])
    (gate,) = _proj(h, w, 3 * d, d, _ep_silu, [(1024, d, BF16)])
    return _hgrn_core(q, lg, v, gate, o_gain, bsz, seq)


def kernel(x, c, ada_w, ada_b, norm_mix_g, norm_ffn_g, dsa_w_in, dsa_q_norm, dsa_kv_norm, dsa_w_out, fox_w_in, fox_b_f, fox_q_norm, fox_k_norm, fox_w_out, hgrn_w_in, hgrn_lb, hgrn_o_norm, hgrn_w_out, ffn_w_in, ffn_conv_w, ffn_conv_b, ffn_w_out):
    bsz, seq, d = x.shape
    depth = ada_w.shape[0]
    f = ffn_conv_b.shape[1]

    mod = _ada_all(c, ada_w, ada_b)
    lb_soft = jax.nn.softmax(hgrn_lb.astype(F32), axis=0)
    lb_all = jnp.cumsum(lb_soft, axis=0) - lb_soft[0]
    conv_b = ffn_conv_b.reshape(depth, 1, f)

    w_outs = {0: dsa_w_out.astype(BF16), 1: fox_w_out.astype(BF16), 2: hgrn_w_out.astype(BF16)}
    dsa_w_in_t = jnp.swapaxes(dsa_w_in, 1, 2)
    fox_w_in_t = jnp.swapaxes(fox_w_in, 1, 2)

    x2 = x.reshape(bsz * seq, d)
    for i in range(depth):
        sh1, sc1, g1, sh2, sc2, g2 = [mod[i, :, j * d:(j + 1) * d] for j in range(6)]
        h = _norm_mod(x2, norm_mix_g[i], sh1, sc1, seq)
        kind, j = i % 3, i // 3
        if kind == 0:
            a = _dsa_mixer(h, dsa_w_in_t, j, dsa_q_norm[j], dsa_kv_norm[j], bsz, seq)
        elif kind == 1:
            a = _fox_mixer(h, fox_w_in_t, j, fox_b_f[j], fox_q_norm[j], fox_k_norm[j], bsz, seq)
        else:
            a = _hgrn_mixer(h, hgrn_w_in[j], lb_all[i], hgrn_o_norm[j], bsz, seq)
        x2 = _out_proj(a, w_outs[kind], j, x2, g1, seq, tm=1024)
        h = _norm_mod(x2, norm_ffn_g[i], sh2, sc2, seq)
        a, w_out_b = _ffn_in(h, ffn_w_in, ffn_conv_w, conv_b, ffn_w_out, i, seq)
        x2 = _out_proj(a, w_out_b, 0, x2, g2, seq, tm=512)
    return x2.reshape(bsz, seq, d)
```

```python
import functools

import jax
import jax.numpy as jnp
from jax import lax
from jax.experimental import pallas as pl
from jax.experimental.pallas import tpu as pltpu

F32 = jnp.float32
BF16 = jnp.bfloat16

EPS = 1e-6
CHUNK = 64
TOPK_MAX = 256

DSA_HEADS = 16
DSA_LATENT = 256
IDX_HEADS = 16
IDX_DIM = 64
FOX_HEADS = 16
HGRN_HEADS = 16
HEAD_DIM = 128
CONV_WIDTH = 3

NEG_BIG = -1e30
LOG2E = 1.4426950408889634
ONES_ROWS = 16
INT_MIN = -2 ** 31

V7X_VMEM_BYTES = 64 * 1024 * 1024
VMEM_LIMIT = 56 * 1024 * 1024


def _params(*sem):
    return pltpu.CompilerParams(dimension_semantics=sem, vmem_limit_bytes=VMEM_LIMIT)


def _dot(a, b):
    return jnp.dot(a, b, preferred_element_type=F32)


def _dot_nt(a, b):
    return lax.dot_general(a, b, (((1,), (1,)), ((), ())), preferred_element_type=F32)


def _dot_tn(a, b):
    return lax.dot_general(a, b, (((0,), (0,)), ((), ())), preferred_element_type=F32)


def _silu(x):
    return x * jax.nn.sigmoid(x)


def _ada_kernel(c_ref, w_ref, b_ref, o_ref):
    cond = _silu(c_ref[...]).astype(BF16)
    o_ref[0] = _dot(cond, w_ref[0].astype(BF16)) + b_ref[0]


def _ada_all(c, ada_w, ada_b, tn=1024):
    depth, d, n = ada_w.shape
    bsz = c.shape[0]
    rows = 8
    c_pad = jnp.zeros((rows, d), F32).at[:bsz].set(c)
    out = pl.pallas_call(
        _ada_kernel,
        grid=(depth, n // tn),
        in_specs=[
            pl.BlockSpec((rows, d), lambda l, j: (0, 0)),
            pl.BlockSpec((1, d, tn), lambda l, j: (l, 0, j)),
            pl.BlockSpec((1, 1, tn), lambda l, j: (l, 0, j)),
        ],
        out_specs=pl.BlockSpec((1, rows, tn), lambda l, j: (l, 0, j)),
        out_shape=jax.ShapeDtypeStruct((depth, rows, n), F32),
        compiler_params=_params("parallel", "parallel"),
        name="ada_mod",
    )(c_pad, ada_w, ada_b.reshape(depth, 1, n))
    return out[:, :bsz]


def _norm_mod_kernel(x_ref, g_ref, sh_ref, sc_ref, h_ref):
    x = x_ref[...]
    ms = jnp.mean(x * x, axis=-1, keepdims=True)
    y = x * lax.rsqrt(ms + EPS) * g_ref[...]
    h_ref[...] = (y * (1.0 + sc_ref[0]) + sh_ref[0]).astype(BF16)


def _norm_mod(x2, g, shift, scale, seq, tm=512):
    t, d = x2.shape
    per_seq = seq // tm
    bsz = shift.shape[0]
    vec = lambda m: (m // per_seq, 0, 0)
    return pl.pallas_call(
        _norm_mod_kernel,
        grid=(t // tm,),
        in_specs=[
            pl.BlockSpec((tm, d), lambda m: (m, 0)),
            pl.BlockSpec((1, d), lambda m: (0, 0)),
            pl.BlockSpec((1, 1, d), vec),
            pl.BlockSpec((1, 1, d), vec),
        ],
        out_specs=pl.BlockSpec((tm, d), lambda m: (m, 0)),
        out_shape=jax.ShapeDtypeStruct((t, d), BF16),
        compiler_params=_params("parallel"),
        name="norm_mod",
    )(x2, g.reshape(1, d), shift.reshape(bsz, 1, d), scale.reshape(bsz, 1, d))


def _proj_kernel(*refs, epilogue, n_extra, cast_w, w_t):
    h_ref, w_ref = refs[0], refs[1]
    extras = refs[2:2 + n_extra]
    if cast_w:
        outs, wb_ref = refs[2 + n_extra:-1], refs[-1]

        @pl.when(pl.program_id(1) == 0)
        def _():
            w32 = w_ref[0].T if w_t else w_ref[...]
            wb_ref[...] = w32.astype(BF16)

        w = wb_ref[...]
    else:
        outs = refs[2 + n_extra:]
        w = w_ref[...]
    acc = _dot(h_ref[...], w)
    epilogue(acc, [e[...] for e in extras], outs)


def _proj(h, w, col0, n, epilogue, out_defs, extras=(), row_extras=(), tm=1024, tn=1024, layer=None):
    t, k = h.shape
    tn = min(tn, n)
    tm = min(tm, t)
    assert n % tn == 0 and col0 % tn == 0 and t % tm == 0
    off = col0 // tn
    in_specs = [
        pl.BlockSpec((tm, k), lambda j, m: (m, 0)),
        pl.BlockSpec((k, tn), lambda j, m: (0, j + off)) if layer is None else
        pl.BlockSpec((1, tn, k), lambda j, m: (layer, j + off, 0)),
    ] + [pl.BlockSpec((1, tn), lambda j, m: (0, j)) for _ in extras
         ] + [pl.BlockSpec((tm, r.shape[1]), lambda j, m: (m, 0)) for r in row_extras]
    extras = tuple(extras) + tuple(row_extras)
    out_specs = [pl.BlockSpec((tm, wt), lambda j, m: (m, j)) for wt, _, _ in out_defs]
    out_shape = [jax.ShapeDtypeStruct((t, wtot), dt) for _, wtot, dt in out_defs]
    cast_w = w.dtype != BF16
    return pl.pallas_call(
        functools.partial(_proj_kernel, epilogue=epilogue, n_extra=len(extras), cast_w=cast_w,
                          w_t=layer is not None),
        grid=(n // tn, t // tm),
        in_specs=in_specs,
        out_specs=out_specs,
        out_shape=out_shape,
        scratch_shapes=[pltpu.VMEM((k, tn), BF16)] if cast_w else [],
        compiler_params=_params("parallel", "arbitrary"),
        name="proj_" + epilogue.__name__.strip("_"),
    )(h, w, *extras)


def _proj_t_kernel(wt_ref, h_ref, o_ref):
    o_ref[...] = _dot_nt(wt_ref[...], h_ref[...]).astype(o_ref.dtype)


def _proj_t(h, wt, tm=1024, tn=1024):
    t, k = h.shape
    n = wt.shape[0]
    tm = min(tm, t)
    return pl.pallas_call(
        _proj_t_kernel,
        grid=(n // tn, t // tm),
        in_specs=[
            pl.BlockSpec((tn, k), lambda j, m: (j, 0)),
            pl.BlockSpec((tm, k), lambda j, m: (m, 0)),
        ],
        out_specs=pl.BlockSpec((tn, tm), lambda j, m: (j, m)),
        out_shape=jax.ShapeDtypeStruct((n, t), BF16),
        compiler_params=_params("parallel", "parallel"),
        name="proj_t",
    )(wt, h)


def _rms_heads(acc, gain_row, hd, scale, o_ref):
    for j in range(acc.shape[1] // hd):
        seg = acc[:, j * hd:(j + 1) * hd]
        ms = jnp.mean(seg * seg, axis=-1, keepdims=True)
        g = gain_row[:, j * hd:(j + 1) * hd]
        if scale != 1.0:
            g = g * scale
        o_ref[:, j * hd:(j + 1) * hd] = (seg * lax.rsqrt(ms + EPS) * g).astype(o_ref.dtype)


def _ep_dsa_q(acc, extras, outs):
    _rms_heads(acc, extras[0], DSA_LATENT, DSA_LATENT ** -0.5 * LOG2E, outs[0])


FOX_AUG = 2 * HEAD_DIM
FOX_BIAS_LANES = 3


def _rms_heads_aug(acc, gain_row, scale, extra_fn, o_ref):
    for j in range(acc.shape[1] // HEAD_DIM):
        cols = slice(j * HEAD_DIM, (j + 1) * HEAD_DIM)
        seg = acc[:, cols]
        ms = jnp.mean(seg * seg, axis=-1, keepdims=True)
        y = seg * lax.rsqrt(ms + EPS) * (gain_row[:, cols] * scale)
        o_ref[:, j * FOX_AUG:j * FOX_AUG + HEAD_DIM] = y.astype(o_ref.dtype)
        o_ref[:, j * FOX_AUG + HEAD_DIM:(j + 1) * FOX_AUG] = extra_fn(j).astype(o_ref.dtype)


def _ep_fox_q(acc, extras, outs):
    lane = lax.broadcasted_iota(jnp.int32, (acc.shape[0], HEAD_DIM), 1)
    ones = jnp.where(lane < FOX_BIAS_LANES, 1.0, 0.0)
    _rms_heads_aug(acc, extras[0], HEAD_DIM ** -0.5 * LOG2E, lambda j: ones, outs[0])


def _ep_fox_k(acc, extras, outs):
    gain_row, cum_t = extras
    heads = acc.shape[1] // HEAD_DIM
    first = pl.program_id(0) * heads
    r = lax.broadcasted_iota(jnp.int32, (FOX_HEADS, acc.shape[1]), 0)
    c = lax.broadcasted_iota(jnp.int32, (FOX_HEADS, acc.shape[1]), 1)
    head_col = jnp.where(r == first + c // HEAD_DIM, c % HEAD_DIM, -1)
    extra = jnp.zeros(acc.shape, F32)
    for i, term in enumerate(_split3(cum_t * (-LOG2E))):
        extra = extra + _dot(term, jnp.where(head_col == i, 1.0, 0.0).astype(BF16))
    _rms_heads_aug(acc, gain_row, 1.0,
                   lambda j: extra[:, j * HEAD_DIM:(j + 1) * HEAD_DIM], outs[0])


def _ep_cast(acc, extras, outs):
    outs[0][...] = acc.astype(outs[0].dtype)


def _ep_sigmoid(acc, extras, outs):
    outs[0][...] = jax.nn.sigmoid(acc).astype(outs[0].dtype)


def _ep_silu(acc, extras, outs):
    outs[0][...] = _silu(acc).astype(outs[0].dtype)


def _ep_hgrn_q(acc, extras, outs):
    outs[0][...] = (_silu(acc) * (HEAD_DIM ** -0.5)).astype(outs[0].dtype)


def _ep_hgrn_logf(acc, extras, outs):
    lb = extras[0]
    outs[0][...] = jnp.log(lb + (1.0 - lb) * jax.nn.sigmoid(acc))


_DSA_S_LAT = 0
_DSA_S_QI = DSA_LATENT
_DSA_S_KE = _DSA_S_QI + IDX_HEADS * IDX_DIM
_DSA_S_KO = _DSA_S_KE + 2 * IDX_DIM
_DSA_S_WI = _DSA_S_KO + 2 * IDX_DIM
_DSA_S_N = _DSA_S_WI + 128


def _ep_dsa_small(acc, extras, outs):
    lat_ref, qi_ref, ke_ref, ko_ref, wi_ref = outs
    _rms_heads(acc[:, :DSA_LATENT], extras[0][:, :DSA_LATENT], DSA_LATENT, 1.0, lat_ref)
    qi_ref[...] = acc[:, _DSA_S_QI:_DSA_S_KE].astype(BF16)
    ke_ref[...] = acc[:, _DSA_S_KE:_DSA_S_KO].astype(BF16)
    ko_ref[...] = acc[:, _DSA_S_KO:_DSA_S_WI].astype(BF16)
    wi_ref[...] = acc[:, _DSA_S_WI:_DSA_S_WI + IDX_HEADS] * (IDX_HEADS ** -0.5 * IDX_DIM ** -0.5)


def _out_kernel(a_ref, w_ref, x_ref, g_ref, o_ref):
    o_ref[...] = x_ref[...] + g_ref[0] * _dot(a_ref[...], w_ref[0])


def _out_proj(a, w_all, layer, x2, gate, seq, tm, tn=1024):
    t, k = a.shape
    d = w_all.shape[2]
    bsz = gate.shape[0]
    per_seq = seq // tm
    return pl.pallas_call(
        _out_kernel,
        grid=(d // tn, t // tm),
        in_specs=[
            pl.BlockSpec((tm, k), lambda j, m: (m, 0)),
            pl.BlockSpec((1, k, tn), lambda j, m: (layer, 0, j)),
            pl.BlockSpec((tm, tn), lambda j, m: (m, j)),
            pl.BlockSpec((1, 1, tn), lambda j, m: (m // per_seq, 0, j)),
        ],
        out_specs=pl.BlockSpec((tm, tn), lambda j, m: (m, j)),
        out_shape=jax.ShapeDtypeStruct((t, d), F32),
        compiler_params=_params("parallel", "parallel"),
        name="out_proj",
    )(a, w_all, x2, gate.reshape(bsz, 1, d))


def _ffn_in_kernel(h_ref, wu_ref, wg_ref, cw_ref, cb_ref, wo_ref, a_ref, wob_ref,
                   carry_ref, wub_ref, wgb_ref, *, per_seq, rc, cc):
    m = pl.program_id(1)
    tm, tn = a_ref.shape

    @pl.when(m == 0)
    def _():
        wub_ref[...] = wu_ref[0].astype(BF16)
        wgb_ref[...] = wg_ref[0].astype(BF16)
        wob_ref[0] = wo_ref[0].astype(BF16)

    @pl.when(m % per_seq == 0)
    def _():
        carry_ref[...] = jnp.zeros_like(carry_ref)

    cw = cw_ref[0]
    cb = cb_ref[0]
    row = lax.broadcasted_iota(jnp.int32, (rc, cc), 0)
    for c in range(tn // cc):
        cols = slice(c * cc, (c + 1) * cc)
        prev = carry_ref[:, cols]
        for r in range(tm // rc):
            rows = slice(r * rc, (r + 1) * rc)
            h = h_ref[rows, :]
            u = _dot(h, wub_ref[:, cols])
            g = _dot(h, wgb_ref[:, cols])
            p1 = prev[7:8, :]
            p2 = prev[6:7, :]
            g1 = jnp.where(row == 0, p1, pltpu.roll(g, 1, axis=0))
            g2 = jnp.where(row == 0, p2, jnp.where(row == 1, p1, pltpu.roll(g, 2, axis=0)))
            conv = cw[0:1, cols] * g2 + cw[1:2, cols] * g1 + cw[2:3, cols] * g + cb[:, cols]
            a_ref[rows, cols] = (_silu(conv) * u).astype(BF16)
            prev = g[rc - 8:, :]
        carry_ref[:, cols] = prev


def _ffn_in(h, w_all, conv_w, conv_b, w_out_all, layer, seq, tm=2048, tn=512, rc=512, cc=256):
    t, k = h.shape
    f = w_all.shape[2] // 2
    nf = f // tn
    tm = min(tm, seq)
    per_seq = seq // tm
    rc = min(rc, tm)
    return pl.pallas_call(
        functools.partial(_ffn_in_kernel, per_seq=per_seq, rc=rc, cc=cc),
        grid=(nf, t // tm),
        in_specs=[
            pl.BlockSpec((tm, k), lambda j, m: (m, 0)),
            pl.BlockSpec((1, k, tn), lambda j, m: (layer, 0, j)),
            pl.BlockSpec((1, k, tn), lambda j, m: (layer, 0, j + nf)),
            pl.BlockSpec((1, CONV_WIDTH, tn), lambda j, m: (layer, 0, j)),
            pl.BlockSpec((1, 1, tn), lambda j, m: (layer, 0, j)),
            pl.BlockSpec((1, tn, k), lambda j, m: (layer, j, 0)),
        ],
        out_specs=[pl.BlockSpec((tm, tn), lambda j, m: (m, j)),
                   pl.BlockSpec((1, tn, k), lambda j, m: (0, j, 0))],
        out_shape=[jax.ShapeDtypeStruct((t, f), BF16), jax.ShapeDtypeStruct((1, f, k), BF16)],
        scratch_shapes=[
            pltpu.VMEM((8, tn), F32),
            pltpu.VMEM((k, tn), BF16),
            pltpu.VMEM((k, tn), BF16),
        ],
        compiler_params=_params("parallel", "arbitrary"),
        name="ffn_in",
    )(h, w_all, w_all, conv_w, conv_b, w_out_all)


def _dsa_select_kernel(qi_ref, ke_ref, ko_ref, wit_ref, bias_ref, key_ref, cut_ref, *, topk, ts):
    i = pl.program_id(1)
    tq = qi_ref.shape[0]
    seq = ke_ref.shape[0]
    n_tiles = ((i + 1) * tq + ts - 1) // ts
    qi = qi_ref[...]
    qit = jnp.concatenate([qi[:, 128 * j:128 * (j + 1)].astype(F32).T.astype(BF16)
                           for j in range(IDX_HEADS // 2)], axis=1)
    wit = wit_ref[...]
    tpos = i * tq + lax.broadcasted_iota(jnp.int32, (ts, tq), 1)
    first_hidden = (tpos // CHUNK + 1) * CHUNK
    row = lax.broadcasted_iota(jnp.int32, (ts, tq), 0)

    def score_tile(t, carry):
        k0 = pl.multiple_of(t * ts, ts)
        sce = _dot(ke_ref[pl.ds(k0, ts), :], qit)
        sco = _dot(ko_ref[pl.ds(k0, ts), :], qit)
        score = jnp.zeros((ts, tq), F32)
        for j in range(IDX_HEADS // 2):
            cols = slice(j * tq, (j + 1) * tq)
            score = score + wit[2 * j:2 * j + 1] * jnp.maximum(sce[:, cols], 0.0)
            score = score + wit[2 * j + 1:2 * j + 2] * jnp.maximum(sco[:, cols], 0.0)
        score = score + 0.0
        bits = pltpu.bitcast(score, jnp.int32)
        key = jnp.where(bits < 0, bits ^ jnp.int32(0x7FFFFFFF), bits)
        key_ref[pl.ds(k0, ts), :] = jnp.where(k0 + row < first_hidden, key, INT_MIN)
        return carry

    lax.fori_loop(0, n_tiles, score_tile, 0)
    kf = jnp.float32(topk)

    def count(hit_fn):
        def body(t, cnt8):
            k0 = pl.multiple_of(t * ts, ts)
            hit = hit_fn(key_ref[pl.ds(k0, ts), :], k0)
            return cnt8 + jnp.sum(hit.reshape(ts // 8, 8, tq), axis=0)
        cnt8 = lax.fori_loop(0, n_tiles, body, jnp.zeros((8, tq), F32))
        return jnp.sum(cnt8, axis=0, keepdims=True)

    def count_ge(cand):
        return count(lambda key, k0: jnp.where(key >= cand, 1.0, 0.0))

    visible = count_ge(jnp.full((1, tq), INT_MIN + 1, jnp.int32))
    c_pos = count_ge(jnp.zeros((1, tq), jnp.int32))
    thr0 = jnp.where(c_pos >= kf, 0, INT_MIN).astype(jnp.int32)
    cnt0 = jnp.where(c_pos >= kf, c_pos, jnp.float32(seq + 1))

    def bit_step(b, state):
        thr, cnt = state
        cand = thr + lax.shift_left(jnp.int32(1), jnp.int32(30) - b)
        c = count_ge(cand)
        take = c >= kf
        return jnp.where(take, cand, thr), jnp.where(take, c, cnt)

    thr, cnt = lax.fori_loop(0, 31, bit_step, (thr0, cnt0))

    cut_ref[...] = jnp.full((1, tq), seq, jnp.int32)
    tied = jnp.where(cnt > kf, jnp.where(visible > kf, 1.0, 0.0), 0.0)

    @pl.when(jnp.max(tied) > 0.0)
    def _():
        need = kf - count(lambda key, k0: jnp.where(key > thr, 1.0, 0.0))
        nbits = seq.bit_length()

        def idx_step(b, cut):
            cand = cut + lax.shift_left(jnp.int32(1), jnp.int32(nbits - 1) - b)
            c = count(lambda key, k0: jnp.where(key == thr, jnp.where(k0 + row < cand, 1.0, 0.0), 0.0))
            return jnp.where(c <= need, cand, cut)

        cut = lax.fori_loop(0, nbits, idx_step, jnp.zeros((1, tq), jnp.int32))
        cut_ref[...] = jnp.where(tied > 0.0, cut, seq)

    def write_tile(t, carry):
        k0 = pl.multiple_of(t * ts, ts)
        key = key_ref[pl.ds(k0, ts), :]
        tie_keep = jnp.where(k0 + row < cut_ref[...], 0.0, NEG_BIG)
        keep = jnp.where(key > thr, 0.0,
                         jnp.where(key == thr, jnp.where(key > INT_MIN, tie_keep, NEG_BIG), NEG_BIG))
        bias_ref[0, pl.ds(k0, ts), :] = keep.astype(BF16)
        return carry

    def fill_tile(t, carry):
        k0 = pl.multiple_of(t * ts, ts)
        bias_ref[0, pl.ds(k0, ts), :] = jnp.full((ts, tq), NEG_BIG, BF16)
        return carry

    lax.fori_loop(0, n_tiles, write_tile, 0)
    lax.fori_loop(n_tiles, seq // ts, fill_tile, 0)


def _dsa_select(qi, ke, ko, wit, bsz, seq, tq=256, ts=256):
    nq = seq // tq
    topk = min(TOPK_MAX, seq // 4)
    return pl.pallas_call(
        functools.partial(_dsa_select_kernel, topk=topk, ts=ts),
        grid=(bsz, nq),
        in_specs=[
            pl.BlockSpec((tq, IDX_HEADS * IDX_DIM), lambda b, i: (b * nq + i, 0)),
            pl.BlockSpec((seq, 2 * IDX_DIM), lambda b, i: (b, 0)),
            pl.BlockSpec((seq, 2 * IDX_DIM), lambda b, i: (b, 0)),
            pl.BlockSpec((IDX_HEADS, tq), lambda b, i: (0, b * nq + i)),
        ],
        out_specs=pl.BlockSpec((1, seq, tq), lambda b, i: (b, 0, i)),
        out_shape=jax.ShapeDtypeStruct((bsz, seq, seq), BF16),
        scratch_shapes=[pltpu.VMEM((seq, tq), jnp.int32), pltpu.VMEM((1, tq), jnp.int32)],
        compiler_params=_params("parallel", "parallel"),
        name="dsa_select",
    )(qi, ke, ko, wit)


def _dsa_attn_kernel(q_ref, lat_ref, latt_ref, bias_ref, o_ref, m_ref, acc_ref,
                     s0_ref, s1_ref, p0_ref, p1_ref, a0_ref, a1_ref, *, hg, ng, tk):
    i = pl.program_id(1)
    tq = q_ref.shape[0]
    r = DSA_LATENT
    last_tile = lat_ref.shape[0] // tk - 1
    n_pairs = ((i + 1) * tq + 2 * tk - 1) // (2 * tk)
    tpos = i * tq + lax.broadcasted_iota(jnp.int32, (tk, tq), 1)
    row = lax.broadcasted_iota(jnp.int32, (tk, tq), 0)
    for g0 in range(0, DSA_HEADS // hg, ng):
        groups = [range((g0 + n) * hg, (g0 + n + 1) * hg) for n in range(ng)]
        q4t = [jnp.concatenate([q_ref[:, h * r:(h + 1) * r].astype(F32).T.astype(BF16) for h in heads], axis=1)
               for heads in groups]

        def scores(j, s_ref):
            k0 = pl.multiple_of(j * tk, tk)
            lat = lat_ref[pl.ds(k0, tk), :]
            for n in range(ng):
                s_ref[n] = _dot(lat, q4t[n])

        def softmax(j, s_ref, p_ref, a_ref):
            k0 = pl.multiple_of(j * tk, tk)
            dist = jnp.abs(tpos - (k0 + row)).astype(F32)
            bias = bias_ref[0, pl.ds(k0, tk), :].astype(F32)
            for n, heads in enumerate(groups):
                for idx, h in enumerate(heads):
                    cs = slice(idx * tq, (idx + 1) * tq)
                    slope = LOG2E * 2.0 ** (-8.0 * (h + 1) / DSA_HEADS)
                    s = s_ref[n, :, cs] + (bias - slope * dist)
                    m_old = m_ref[n, :, cs]
                    m_new = jnp.maximum(m_old, jnp.max(s, axis=0, keepdims=True))
                    m_ref[n, :, cs] = m_new
                    p_ref[n, :, cs] = jnp.exp2(s - m_new).astype(BF16)
                    a_ref[n, :, cs] = jnp.exp2(m_old - m_new)

        def values(j, p_ref, a_ref):
            k0 = pl.multiple_of(j * tk, tk)
            latt = jnp.concatenate([latt_ref[:, pl.ds(k0, tk)], jnp.ones((ONES_ROWS, tk), BF16)], axis=0)
            pvs = [_dot(latt, p_ref[n]) for n in range(ng)]
            for n in range(ng):
                acc_ref[n] = a_ref[n] * acc_ref[n] + pvs[n]

        m_ref[...] = jnp.full(m_ref.shape, NEG_BIG, F32)
        acc_ref[...] = jnp.zeros(acc_ref.shape, F32)
        scores(0, s0_ref)

        def trip(ta, first):
            scores(ta + 1, s1_ref)
            if not first:
                values(ta - 1, p1_ref, a1_ref)
            softmax(ta, s0_ref, p0_ref, a0_ref)
            scores(jnp.minimum(ta + 2, last_tile), s0_ref)
            values(ta, p0_ref, a0_ref)
            softmax(ta + 1, s1_ref, p1_ref, a1_ref)

        def body(jj, carry):
            trip(2 * jj, False)
            return carry

        trip(0, True)
        lax.fori_loop(1, n_pairs, body, 0)
        values(2 * n_pairs - 1, p1_ref, a1_ref)
        for n, heads in enumerate(groups):
            out = (acc_ref[n, :r, :] / acc_ref[n, r:r + 1, :]).T
            for idx, h in enumerate(heads):
                o_ref[:, h * r:(h + 1) * r] = out[idx * tq:(idx + 1) * tq].astype(BF16)


def _dsa_attn(q, lat, latt, bias, bsz, seq, tq=128, hg=4, ng=4, tk=256):
    nq = seq // tq
    width = DSA_HEADS * DSA_LATENT
    cols = hg * tq
    assert (seq // tk) % 2 == 0
    return pl.pallas_call(
        functools.partial(_dsa_attn_kernel, hg=hg, ng=ng, tk=tk),
        grid=(bsz, nq),
        in_specs=[
            pl.BlockSpec((tq, width), lambda b, i: (b * nq + i, 0)),
            pl.BlockSpec((seq, DSA_LATENT), lambda b, i: (b, 0)),
            pl.BlockSpec((DSA_LATENT, seq), lambda b, i: (0, b)),
            pl.BlockSpec((1, seq, tq), lambda b, i: (b, 0, i)),
        ],
        out_specs=pl.BlockSpec((tq, width), lambda b, i: (b * nq + i, 0)),
        out_shape=jax.ShapeDtypeStruct((bsz * seq, width), BF16),
        scratch_shapes=[
            pltpu.VMEM((ng, 1, cols), F32),
            pltpu.VMEM((ng, DSA_LATENT + ONES_ROWS, cols), F32),
            pltpu.VMEM((ng, tk, cols), F32),
            pltpu.VMEM((ng, tk, cols), F32),
            pltpu.VMEM((ng, tk, cols), BF16),
            pltpu.VMEM((ng, tk, cols), BF16),
            pltpu.VMEM((ng, 1, cols), F32),
            pltpu.VMEM((ng, 1, cols), F32),
        ],
        compiler_params=_params("parallel", "parallel"),
        name="dsa_attn",
    )(q, lat, latt, bias)


def _split3(x):
    hi = x.astype(BF16)
    r1 = x - hi.astype(F32)
    mid = r1.astype(BF16)
    lo = (r1 - mid.astype(F32)).astype(BF16)
    return hi, mid, lo


def _fox_cum_kernel(h_ref, wt_ref, b_ref, cum_ref, *, blk):
    fl = _dot_nt(wt_ref[...], h_ref[...]) + b_ref[...]
    logf = jax.nn.log_sigmoid(fl)
    seq = logf.shape[1]
    r = lax.broadcasted_iota(jnp.int32, (blk, blk), 0)
    c = lax.broadcasted_iota(jnp.int32, (blk, blk), 1)
    upper = jnp.where(r <= c, 1.0, 0.0).astype(BF16)
    carry = jnp.zeros((logf.shape[0], 1), F32)
    for j in range(seq // blk):
        hi, mid, lo = _split3(logf[:, j * blk:(j + 1) * blk])
        cs = (_dot(hi, upper) + _dot(mid, upper)) + _dot(lo, upper) + carry
        cum_ref[0, :, j * blk:(j + 1) * blk] = cs
        carry = cs[:, blk - 1:blk]


def _fox_cum(h, w_fl_t, b_f, bsz, seq, blk=256):
    d = h.shape[1]
    return pl.pallas_call(
        functools.partial(_fox_cum_kernel, blk=blk),
        grid=(bsz,),
        in_specs=[
            pl.BlockSpec((seq, d), lambda b: (b, 0)),
            pl.BlockSpec((FOX_HEADS, d), lambda b: (0, 0)),
            pl.BlockSpec((FOX_HEADS, 1), lambda b: (0, 0)),
        ],
        out_specs=pl.BlockSpec((1, FOX_HEADS, seq), lambda b: (b, 0, 0)),
        out_shape=jax.ShapeDtypeStruct((bsz, FOX_HEADS, seq), F32),
        compiler_params=_params("parallel"),
        name="fox_cum",
    )(h, w_fl_t, b_f.reshape(FOX_HEADS, 1))


def _fox_attn_kernel(q_ref, k_ref, vt_ref, g_ref, o_ref, m_ref, acc_ref):
    i = pl.program_id(2)
    tq = q_ref.shape[0]
    hb = g_ref.shape[1] // HEAD_DIM
    qts = [q_ref[:, h * FOX_AUG:(h + 1) * FOX_AUG].astype(F32).T.astype(BF16) for h in range(hb)]
    m_ref[...] = jnp.full(m_ref.shape, NEG_BIG, F32)
    acc_ref[...] = jnp.zeros(acc_ref.shape, F32)
    ones = jnp.ones((ONES_ROWS, tq), BF16)

    def tile(j, diagonal):
        k0 = pl.multiple_of(j * tq, tq)
        ss = [_dot(k_ref[pl.ds(k0, tq), h * FOX_AUG:(h + 1) * FOX_AUG], qts[h])
              for h in range(hb)]
        ps, alphas = [], []
        for h in range(hb):
            s = ss[h]
            if diagonal:
                key = lax.broadcasted_iota(jnp.int32, (tq, tq), 0)
                qry = lax.broadcasted_iota(jnp.int32, (tq, tq), 1)
                s = jnp.where(key <= qry, s, NEG_BIG)
            m_old = m_ref[h]
            m_new = jnp.maximum(m_old, jnp.max(s, axis=0, keepdims=True))
            m_ref[h] = m_new
            ps.append(jnp.exp2(s - m_new).astype(BF16))
            alphas.append(jnp.exp2(m_old - m_new))
        pvs = [_dot(jnp.concatenate([vt_ref[h * HEAD_DIM:(h + 1) * HEAD_DIM, pl.ds(k0, tq)], ones], axis=0),
                    ps[h]) for h in range(hb)]
        for h in range(hb):
            acc_ref[h] = alphas[h] * acc_ref[h] + pvs[h]

    def body(j, carry):
        tile(j, False)
        return carry

    lax.fori_loop(0, i, body, 0)
    tile(i, True)
    for h in range(hb):
        cols = slice(h * HEAD_DIM, (h + 1) * HEAD_DIM)
        o = (acc_ref[h, :HEAD_DIM, :] / acc_ref[h, HEAD_DIM:HEAD_DIM + 1, :]).T
        o_ref[:, cols] = (o * g_ref[:, cols].astype(F32)).astype(BF16)


def _fox_attn(q, k, vt, gate, bsz, seq, tq=256, hb=8):
    nq = seq // tq
    d = gate.shape[1]
    qmap = lambda b, h, i: (b * nq + i, h)
    return pl.pallas_call(
        _fox_attn_kernel,
        grid=(bsz, FOX_HEADS // hb, nq),
        in_specs=[
            pl.BlockSpec((tq, hb * FOX_AUG), qmap),
            pl.BlockSpec((seq, hb * FOX_AUG), lambda b, h, i: (b, h)),
            pl.BlockSpec((hb * HEAD_DIM, seq), lambda b, h, i: (h, b)),
            pl.BlockSpec((tq, hb * HEAD_DIM), qmap),
        ],
        out_specs=pl.BlockSpec((tq, hb * HEAD_DIM), qmap),
        out_shape=jax.ShapeDtypeStruct((bsz * seq, d), BF16),
        scratch_shapes=[
            pltpu.VMEM((hb, 1, tq), F32),
            pltpu.VMEM((hb, HEAD_DIM + ONES_ROWS, tq), F32),
        ],
        compiler_params=_params("parallel", "parallel", "parallel"),
        name="fox_attn",
    )(q, k, vt, gate)


HGRN_SUB = 8


def _hgrn_kernel(q_ref, lg_ref, v_ref, gate_ref, gain_ref, o_ref,
                 state_ref, oacc_ref, kpad_ref, fpad_ref, vpad_ref, *, hb):
    c = pl.program_id(2)
    cl = q_ref.shape[0]
    dk = HEAD_DIM
    sub = HGRN_SUB

    @pl.when(c == 0)
    def _():
        state_ref[...] = jnp.zeros_like(state_ref)
        kpad_ref[0:sub, :] = jnp.zeros((sub, dk), F32)
        fpad_ref[0:sub, :] = jnp.zeros((sub, dk), F32)
        vpad_ref[0:sub, :] = jnp.zeros((sub, dk), F32)

    r = lax.broadcasted_iota(jnp.int32, (cl, cl), 0)
    cc = lax.broadcasted_iota(jnp.int32, (cl, cl), 1)
    lower = jnp.where(cc <= r, 1.0, 0.0).astype(BF16)
    block_start = lax.broadcasted_iota(jnp.int32, (cl, dk), 0) % sub == 0

    for hh in range(hb):
        cols = slice(hh * dk, (hh + 1) * dk)
        q = q_ref[:, cols].astype(F32)
        lg = lg_ref[:, cols]
        vb = v_ref[:, cols]
        v = vb.astype(F32)
        hi, mid, lo = _split3(lg)
        gcum = (_dot(lower, hi) + _dot(lower, mid)) + _dot(lower, lo)
        f = jnp.exp(lg)
        k = 1.0 - f
        st = state_ref[hh]

        oacc_ref[...] = _dot_nt((q * jnp.exp(gcum)).astype(BF16), st.astype(BF16))

        m = cl // 2
        while m >= sub:
            for blk in range(cl // (2 * m)):
                r0 = blk * 2 * m
                mid_row = r0 + m
                gm = gcum[mid_row - 1:mid_row, :]
                qs = (q[mid_row:mid_row + m] * jnp.exp(gcum[mid_row:mid_row + m] - gm)).astype(BF16)
                ks = (k[r0:mid_row] * jnp.exp(gm - gcum[r0:mid_row])).astype(BF16)
                sc = _dot_nt(qs, ks)
                oacc_ref[mid_row:mid_row + m, :] += _dot(sc.astype(BF16), vb[r0:mid_row])
            m //= 2

        kpad_ref[sub:, :] = k
        fpad_ref[sub:, :] = jnp.where(block_start, 0.0, f)
        vpad_ref[sub:, :] = v
        diag = jnp.sum(q * k, axis=-1, keepdims=True) * v
        dec = None
        for delta in range(1, sub):
            fd = fpad_ref[sub - delta + 1:sub - delta + 1 + cl, :]
            dec = fd if dec is None else dec * fd
            kd = kpad_ref[sub - delta:sub - delta + cl, :]
            vd = vpad_ref[sub - delta:sub - delta + cl, :]
            diag = diag + jnp.sum(q * kd * dec, axis=-1, keepdims=True) * vd
        o = oacc_ref[...] + diag

        gl = gcum[cl - 1:cl, :]
        kdec = (k * jnp.exp(gl - gcum)).astype(BF16)
        state_ref[hh] = st * jnp.exp(gl) + _dot_tn(vb, kdec)

        ms = jnp.mean(o * o, axis=-1, keepdims=True)
        y = o * lax.rsqrt(ms + EPS) * gain_ref[...]
        o_ref[:, cols] = (y * gate_ref[:, cols].astype(F32)).astype(BF16)


def _hgrn_core(q, lg, v, gate, gain, bsz, seq, cl=256, hb=8):
    nc = seq // cl
    d = q.shape[1]
    blk = lambda b, g, c: (b * nc + c, g)
    spec = pl.BlockSpec((cl, hb * HEAD_DIM), blk)
    return pl.pallas_call(
        functools.partial(_hgrn_kernel, hb=hb),
        grid=(bsz, HGRN_HEADS // hb, nc),
        in_specs=[spec, spec, spec, spec, pl.BlockSpec((1, HEAD_DIM), lambda b, g, c: (0, 0))],
        out_specs=spec,
        out_shape=jax.ShapeDtypeStruct((bsz * seq, d), BF16),
        scratch_shapes=[
            pltpu.VMEM((hb, HEAD_DIM, HEAD_DIM), F32),
            pltpu.VMEM((cl, HEAD_DIM), F32),
            pltpu.VMEM((cl + HGRN_SUB, HEAD_DIM), F32),
            pltpu.VMEM((cl + HGRN_SUB, HEAD_DIM), F32),
            pltpu.VMEM((cl + HGRN_SUB, HEAD_DIM), F32),
        ],
        compiler_params=_params("parallel", "parallel", "arbitrary"),
        name="hgrn_core",
    )(q, lg, v, gate, gain.reshape(1, HEAD_DIM))


def _dsa_mixer(h, w_in_t, layer, q_gain, kv_gain, bsz, seq):
    d = h.shape[1]
    nq = DSA_HEADS * DSA_LATENT
    n_qi = IDX_HEADS * IDX_DIM
    small = lax.optimization_barrier(w_in_t[layer, nq:])
    lat_w = small[:DSA_LATENT]
    qi_w = small[DSA_LATENT:DSA_LATENT + n_qi]
    ki_w = small[DSA_LATENT + n_qi:DSA_LATENT + n_qi + IDX_DIM]
    wi_w = small[DSA_LATENT + n_qi + IDX_DIM:]
    z = jnp.zeros((IDX_DIM, d), F32)
    w_s = jnp.concatenate(
        [lat_w, qi_w, ki_w, z, z, ki_w, wi_w, jnp.zeros((128 - IDX_HEADS, d), F32)], axis=0).T
    q_gain_row = jnp.tile(q_gain, DSA_HEADS).reshape(1, nq)
    kv_gain_row = jnp.zeros((1, _DSA_S_N), F32).at[0, :DSA_LATENT].set(kv_gain)

    (q,) = _proj(h, w_in_t, 0, nq, _ep_dsa_q, [(1024, nq, BF16)], extras=(q_gain_row,), layer=layer)
    lat, qi, ke, ko, wi = _proj(
        h, w_s, 0, _DSA_S_N, _ep_dsa_small,
        [(DSA_LATENT, DSA_LATENT, BF16), (IDX_HEADS * IDX_DIM, IDX_HEADS * IDX_DIM, BF16),
         (2 * IDX_DIM, 2 * IDX_DIM, BF16), (2 * IDX_DIM, 2 * IDX_DIM, BF16), (IDX_HEADS, IDX_HEADS, F32)],
        extras=(kv_gain_row,), tn=_DSA_S_N)
    bias = _dsa_select(qi, ke, ko, wi.T, bsz, seq)
    return _dsa_attn(q, lat, lat.T, bias, bsz, seq)


def _fox_mixer(h, w_in_t, layer, b_f, q_gain, k_gain, bsz, seq):
    d = h.shape[1]
    w_fl_t = lax.optimization_barrier(w_in_t[layer, 4 * d:]).astype(BF16)
    qg = jnp.tile(q_gain, FOX_HEADS).reshape(1, d)
    kg = jnp.tile(k_gain, FOX_HEADS).reshape(1, d)
    cum = _fox_cum(h, w_fl_t, b_f, bsz, seq)
    cum_t = cum.transpose(0, 2, 1).reshape(bsz * seq, FOX_HEADS)
    aug = [(1024 // HEAD_DIM * FOX_AUG, FOX_HEADS * FOX_AUG, BF16)]
    (q,) = _proj(h, w_in_t, 0, d, _ep_fox_q, aug, extras=(qg,), layer=layer)
    (k,) = _proj(h, w_in_t, d, d, _ep_fox_k, aug, extras=(kg,), row_extras=(cum_t,), layer=layer)
    vt = _proj_t(h, lax.optimization_barrier(w_in_t[layer, 2 * d:3 * d]).astype(BF16))
    (gate,) = _proj(h, w_in_t, 3 * d, d, _ep_sigmoid, [(1024, d, BF16)], layer=layer)
    return _fox_attn(q, k, vt, gate, bsz, seq)


def _hgrn_mixer(h, w_in, lb, o_gain, bsz, seq):
    d = h.shape[1]
    w = w_in
    (q,) = _proj(h, w, 0, d, _ep_hgrn_q, [(1024, d, BF16)])
    (lg,) = _proj(h, w, d, d, _ep_hgrn_logf, [(1024, d, F32)], extras=(lb.reshape(1, d),))
    (v,) = _proj(h, w, 2 * d, d, _ep_cast, [(1024, d, BF16)])
    (gate,) = _proj(h, w, 3 * d, d, _ep_silu, [(1024, d, BF16)])
    return _hgrn_core(q, lg, v, gate, o_gain, bsz, seq)


def kernel(x, c, ada_w, ada_b, norm_mix_g, norm_ffn_g, dsa_w_in, dsa_q_norm, dsa_kv_norm, dsa_w_out, fox_w_in, fox_b_f, fox_q_norm, fox_k_norm, fox_w_out, hgrn_w_in, hgrn_lb, hgrn_o_norm, hgrn_w_out, ffn_w_in, ffn_conv_w, ffn_conv_b, ffn_w_out):
    bsz, seq, d = x.shape
    depth = ada_w.shape[0]
    f = ffn_conv_b.shape[1]

    mod = _ada_all(c, ada_w, ada_b)
    lb_soft = jax.nn.softmax(hgrn_lb.astype(F32), axis=0)
    lb_all = jnp.cumsum(lb_soft, axis=0) - lb_soft[0]
    conv_b = ffn_conv_b.reshape(depth, 1, f)

    w_outs = {0: dsa_w_out.astype(BF16), 1: fox_w_out.astype(BF16), 2: hgrn_w_out.astype(BF16)}
    dsa_w_in_t = jnp.swapaxes(dsa_w_in, 1, 2)
    fox_w_in_t = jnp.swapaxes(fox_w_in, 1, 2)

    x2 = x.reshape(bsz * seq, d)
    for i in range(depth):
        sh1, sc1, g1, sh2, sc2, g2 = [mod[i, :, j * d:(j + 1) * d] for j in range(6)]
        h = _norm_mod(x2, norm_mix_g[i], sh1, sc1, seq)
        kind, j = i % 3, i // 3
        if kind == 0:
            a = _dsa_mixer(h, dsa_w_in_t, j, dsa_q_norm[j], dsa_kv_norm[j], bsz, seq)
        elif kind == 1:
            a = _fox_mixer(h, fox_w_in_t, j, fox_b_f[j], fox_q_norm[j], fox_k_norm[j], bsz, seq)
        else:
            a = _hgrn_mixer(h, hgrn_w_in[j], lb_all[i], hgrn_o_norm[j], bsz, seq)
        x2 = _out_proj(a, w_outs[kind], j, x2, g1, seq, tm=1024)
        h = _norm_mod(x2, norm_ffn_g[i], sh2, sc2, seq)
        a, w_out_b = _ffn_in(h, ffn_w_in, ffn_conv_w, conv_b, ffn_w_out, i, seq)
        x2 = _out_proj(a, w_out_b, 0, x2, g2, seq, tm=512)
    return x2.reshape(bsz, seq, d)
```
